```python
import math
import jax, jax.numpy as jnp
from jax import lax
import numpy as np

D_MODEL = 4096
BATCH = 2
SEQ = 4096
DEPTH = 2

CHUNK = 64
Q_BLOCK = 128
EPS = 1e-6
ROPE_THETA = 10000.0

MLA_HEADS = 16
MLA_Q_RANK = 1024
MLA_KV_RANK = 512
MLA_NOPE = 128
MLA_ROPE = 64
MLA_V = 128
SB_HEADS = 16
SB_DIM = 128
DIFF_HEADS = 16
DIFF_DIM = 128
FFN_DIM = 11008
CONV_W = 3

N_EVEN = (DEPTH + 1) // 2
N_ODD = DEPTH // 2
IN_AB = MLA_Q_RANK + MLA_KV_RANK + MLA_ROPE + 3 * SB_HEADS * SB_DIM
OUT_AB = MLA_HEADS * MLA_V + SB_HEADS * SB_DIM
IN_C = 6 * DIFF_HEADS * DIFF_DIM
OUT_C = DIFF_HEADS * 2 * DIFF_DIM

kernel_name = "hybrid_mla_stickbreak_diffattn_convffn"


def rms_norm(x, g):
    xf = x.astype(jnp.float32)
    y = xf * lax.rsqrt(jnp.mean(xf * xf, axis=-1, keepdims=True) + EPS)
    return (y * g.astype(jnp.float32)).astype(x.dtype)


def rope(x, pos):
    half = x.shape[-1] // 2
    inv = ROPE_THETA ** (-jnp.arange(half, dtype=jnp.float32) / half)
    ang = pos.astype(jnp.float32)[:, None] * inv[None, :]
    cos, sin = jnp.cos(ang), jnp.sin(ang)
    x1 = x[..., :half].astype(jnp.float32)
    x2 = x[..., half:].astype(jnp.float32)
    out = jnp.concatenate([x1 * cos - x2 * sin, x2 * cos + x1 * sin], axis=-1)
    return out.astype(x.dtype)


def to_blocks(a):
    b, h, s, d = a.shape
    return jnp.moveaxis(a.reshape(b, h, s // Q_BLOCK, Q_BLOCK, d), 2, 0)


def from_blocks(a):
    nb, b, h, qb, d = a.shape
    return jnp.moveaxis(a, 0, 2).reshape(b, h, nb * qb, d)


def chunk_mask(t_pos, s_pos):
    return (s_pos[None, :] // CHUNK) <= (t_pos[:, None] // CHUNK)


def alibi_slopes(n):
    return jnp.asarray([2.0 ** (-8.0 * (h + 1) / n) for h in range(n)], dtype=jnp.float32)


def mla_attention(q_nope, q_rope, k_nope, k_rope, v):
    seq = k_nope.shape[2]
    s_pos = jnp.arange(seq)
    scale = 1.0 / math.sqrt(MLA_NOPE + MLA_ROPE)

    def block(args):
        blk, qn, qr = args
        t_pos = blk * Q_BLOCK + jnp.arange(Q_BLOCK)
        s = (jnp.einsum('bhqd,bhkd->bhqk', qn, k_nope).astype(jnp.float32)
             + jnp.einsum('bhqr,bkr->bhqk', qr, k_rope).astype(jnp.float32)) * scale
        s = jnp.where(chunk_mask(t_pos, s_pos)[None, None], s, -1e30)
        p = jax.nn.softmax(s, axis=-1)
        return jnp.einsum('bhqk,bhkd->bhqd', p.astype(v.dtype), v)

    nb = seq // Q_BLOCK
    out = lax.map(block, (jnp.arange(nb), to_blocks(q_nope), to_blocks(q_rope)))
    return from_blocks(out)


def stick_breaking_attention(q, k, v):
    seq = k.shape[2]
    s_pos = jnp.arange(seq)
    scale = 1.0 / math.sqrt(SB_DIM)

    def block(args):
        blk, qb = args
        t_pos = blk * Q_BLOCK + jnp.arange(Q_BLOCK)
        strict = (s_pos[None, :] < t_pos[:, None])[None, None]
        z = jnp.einsum('bhqd,bhkd->bhqk', qb, k).astype(jnp.float32) * scale
        log_keep = jnp.where(strict, jax.nn.log_sigmoid(-z), 0.0)
        after = lax.cumsum(log_keep, axis=3, reverse=True) - log_keep
        log_a = jax.nn.log_sigmoid(z) + after
        a = jnp.where(strict, jnp.exp(log_a), 0.0)
        return jnp.einsum('bhqk,bhkd->bhqd', a.astype(v.dtype), v)

    nb = seq // Q_BLOCK
    out = lax.map(block, (jnp.arange(nb), to_blocks(q)))
    return from_blocks(out)


def differential_attention(q1, q2, k1, k2, v, lam):
    seq = k1.shape[2]
    s_pos = jnp.arange(seq)
    scale = 1.0 / math.sqrt(DIFF_DIM)
    slopes = alibi_slopes(DIFF_HEADS)

    def block(args):
        blk, qa, qb = args
        t_pos = blk * Q_BLOCK + jnp.arange(Q_BLOCK)
        dist = jnp.abs(t_pos[:, None] - s_pos[None, :]).astype(jnp.float32)
        bias = -slopes[:, None, None] * dist
        mask = chunk_mask(t_pos, s_pos)[None, None]
        s1 = jnp.einsum('bhqd,bhkd->bhqk', qa, k1).astype(jnp.float32) * scale + bias[None]
        s2 = jnp.einsum('bhqd,bhkd->bhqk', qb, k2).astype(jnp.float32) * scale + bias[None]
        p1 = jax.nn.softmax(jnp.where(mask, s1, -1e30), axis=-1)
        p2 = jax.nn.softmax(jnp.where(mask, s2, -1e30), axis=-1)
        a = p1 - lam * p2
        return jnp.einsum('bhqk,bhkd->bhqd', a.astype(v.dtype), v)

    nb = seq // Q_BLOCK
    out = lax.map(block, (jnp.arange(nb), to_blocks(q1), to_blocks(q2)))
    return from_blocks(out)


def mla_sb_mixer(x, w_in, q_norm, w_uq, kv_norm, w_ukv, w_out):
    b, s, _ = x.shape
    pos = jnp.arange(s)
    proj = x @ w_in
    o1 = MLA_Q_RANK
    o2 = o1 + MLA_KV_RANK
    o3 = o2 + MLA_ROPE
    c_q, c_kv, k_r, sb = proj[..., :o1], proj[..., o1:o2], proj[..., o2:o3], proj[..., o3:]

    q = (rms_norm(c_q, q_norm) @ w_uq).reshape(b, s, MLA_HEADS, MLA_NOPE + MLA_ROPE)
    q = jnp.transpose(q, (0, 2, 1, 3))
    q_nope, q_rope = q[..., :MLA_NOPE], rope(q[..., MLA_NOPE:], pos)
    kv = (rms_norm(c_kv, kv_norm) @ w_ukv).reshape(b, s, MLA_HEADS, MLA_NOPE + MLA_V)
    kv = jnp.transpose(kv, (0, 2, 1, 3))
    k_nope, v_a = kv[..., :MLA_NOPE], kv[..., MLA_NOPE:]
    k_rope = rope(k_r, pos)
    o_a = mla_attention(q_nope, q_rope, k_nope, k_rope, v_a)

    sb = jnp.transpose(sb.reshape(b, s, 3, SB_HEADS, SB_DIM), (2, 0, 3, 1, 4))
    o_b = stick_breaking_attention(sb[0], sb[1], sb[2])

    o = jnp.concatenate([o_a, o_b], axis=1)
    o = jnp.transpose(o, (0, 2, 1, 3)).reshape(b, s, OUT_AB)
    return o @ w_out


def diff_mixer(x, w_in, lq1, lk1, lq2, lk2, subln, w_out, layer_idx):
    b, s, _ = x.shape
    lambda_init = 0.8 - 0.6 * math.exp(-0.3 * layer_idx)
    proj = x @ w_in
    qk_w = DIFF_HEADS * 2 * DIFF_DIM
    q = jnp.transpose(proj[..., :qk_w].reshape(b, s, DIFF_HEADS, 2, DIFF_DIM), (3, 0, 2, 1, 4))
    k = jnp.transpose(proj[..., qk_w:2 * qk_w].reshape(b, s, DIFF_HEADS, 2, DIFF_DIM), (3, 0, 2, 1, 4))
    v = jnp.transpose(proj[..., 2 * qk_w:].reshape(b, s, DIFF_HEADS, 2 * DIFF_DIM), (0, 2, 1, 3))
    lam = (jnp.exp(jnp.sum(lq1.astype(jnp.float32) * lk1.astype(jnp.float32)))
           - jnp.exp(jnp.sum(lq2.astype(jnp.float32) * lk2.astype(jnp.float32)))
           + lambda_init)
    o = differential_attention(q[0], q[1], k[0], k[1], v, lam)
    o = rms_norm(o, subln) * (1.0 - lambda_init)
    o = jnp.transpose(o, (0, 2, 1, 3)).reshape(b, s, OUT_C)
    return o @ w_out


def conv_ffn(x, w_up, conv_w, conv_b, w_down):
    s = x.shape[1]
    h = x @ w_up
    hp = jnp.pad(h, ((0, 0), (CONV_W - 1, 0), (0, 0)))
    h = sum(hp[:, j:j + s] * conv_w[j] for j in range(CONV_W)) + conv_b
    g, u = h[..., :FFN_DIM], h[..., FFN_DIM:]
    return (jax.nn.silu(g) * u) @ w_down


def setup_inputs(seed: int = 0) -> dict:
    key = jax.random.key(seed)
    ks = jax.random.split(key, 24)
    f32 = jnp.float32

    def nrm(k, shape, fan_in):
        return jax.random.normal(k, shape, f32) * (fan_in ** -0.5)

    def gain(k, shape):
        return 1.0 + 0.02 * jax.random.normal(k, shape, f32)

    return {
        "x": jax.random.normal(ks[0], (BATCH, SEQ, D_MODEL), f32),
        "norm_mix": gain(ks[1], (DEPTH, D_MODEL)),
        "norm_ffn": gain(ks[2], (DEPTH, D_MODEL)),
        "norm_final": gain(ks[3], (D_MODEL,)),
        "ab_w_in": nrm(ks[4], (N_EVEN, D_MODEL, IN_AB), D_MODEL),
        "ab_q_norm": gain(ks[5], (N_EVEN, MLA_Q_RANK)),
        "ab_w_uq": nrm(ks[6], (N_EVEN, MLA_Q_RANK, MLA_HEADS * (MLA_NOPE + MLA_ROPE)), MLA_Q_RANK),
        "ab_kv_norm": gain(ks[7], (N_EVEN, MLA_KV_RANK)),
        "ab_w_ukv": nrm(ks[8], (N_EVEN, MLA_KV_RANK, MLA_HEADS * (MLA_NOPE + MLA_V)), MLA_KV_RANK),
        "ab_w_out": nrm(ks[9], (N_EVEN, OUT_AB, D_MODEL), OUT_AB),
        "c_w_in": nrm(ks[10], (N_ODD, D_MODEL, IN_C), D_MODEL),
        "c_lambda_q1": 0.1 * jax.random.normal(ks[11], (N_ODD, DIFF_DIM), f32),
        "c_lambda_k1": 0.1 * jax.random.normal(ks[12], (N_ODD, DIFF_DIM), f32),
        "c_lambda_q2": 0.1 * jax.random.normal(ks[13], (N_ODD, DIFF_DIM), f32),
        "c_lambda_k2": 0.1 * jax.random.normal(ks[14], (N_ODD, DIFF_DIM), f32),
        "c_subln": gain(ks[15], (N_ODD, 2 * DIFF_DIM)),
        "c_w_out": nrm(ks[16], (N_ODD, OUT_C, D_MODEL), OUT_C),
        "ffn_w_up": nrm(ks[17], (DEPTH, D_MODEL, 2 * FFN_DIM), D_MODEL),
        "ffn_conv_w": nrm(ks[18], (DEPTH, CONV_W, 2 * FFN_DIM), CONV_W),
        "ffn_conv_b": 0.02 * jax.random.normal(ks[19], (DEPTH, 2 * FFN_DIM), f32),
        "ffn_w_down": nrm(ks[20], (DEPTH, FFN_DIM, D_MODEL), FFN_DIM),
    }


def reference(x, norm_mix, norm_ffn, norm_final, ab_w_in, ab_q_norm, ab_w_uq, ab_kv_norm,
              ab_w_ukv, ab_w_out, c_w_in, c_lambda_q1, c_lambda_k1, c_lambda_q2, c_lambda_k2,
              c_subln, c_w_out, ffn_w_up, ffn_conv_w, ffn_conv_b, ffn_w_down):
    h = x
    for i in range(DEPTH):
        j = i // 2
        hn = rms_norm(h, norm_mix[i])
        if i % 2 == 0:
            h = h + mla_sb_mixer(hn, ab_w_in[j], ab_q_norm[j], ab_w_uq[j], ab_kv_norm[j],
                                 ab_w_ukv[j], ab_w_out[j])
        else:
            h = h + diff_mixer(hn, c_w_in[j], c_lambda_q1[j], c_lambda_k1[j], c_lambda_q2[j],
                               c_lambda_k2[j], c_subln[j], c_w_out[j], i)
        hn = rms_norm(h, norm_ffn[i])
        h = h + conv_ffn(hn, ffn_w_up[i], ffn_conv_w[i], ffn_conv_b[i], ffn_w_down[i])
    return rms_norm(h, norm_final)
```

```python
import functools
import math

import jax
import jax.numpy as jnp
from jax import lax
from jax.experimental import pallas as pl
from jax.experimental.pallas import tpu as pltpu

F32 = jnp.float32
BF16 = jnp.bfloat16

D_MODEL = 4096
SEQ = 4096
CHUNK = 64
EPS = 1e-6
ROPE_THETA = 10000.0
MLA_HEADS = 16
MLA_Q_RANK = 1024
MLA_KV_RANK = 512
MLA_NOPE = 128
MLA_ROPE = 64
MLA_V = 128
SB_HEADS = 16
SB_DIM = 128
DIFF_HEADS = 16
DIFF_DIM = 128
FFN_DIM = 11008
CONV_W = 3

LANES = 128
MLA_QK_PAD = 256
ATT_TQ = 256
ATT_TK = 256
FFN_TN = 256
MIB = 1024 * 1024


def _params(sem, vmem_mib):
    return pltpu.CompilerParams(dimension_semantics=sem, vmem_limit_bytes=vmem_mib * MIB)


def _rms(x, g):
    return x * lax.rsqrt(jnp.mean(x * x, axis=-1, keepdims=True) + EPS) * g


def _rope_slab(x, c, s1, s2):
    return x * c + pltpu.roll(x, 96, 1) * s1 + pltpu.roll(x, 32, 1) * s2


def _rmsnorm_kernel(x_ref, g_ref, o_ref):
    o_ref[...] = _rms(x_ref[...].astype(F32), g_ref[...]).astype(o_ref.dtype)


def _rmsnorm(x, g, out_dtype, tm=256):
    m, d = x.shape
    return pl.pallas_call(
        _rmsnorm_kernel,
        grid=(m // tm,),
        in_specs=[pl.BlockSpec((tm, d), lambda i: (i, 0)),
                  pl.BlockSpec((1, d), lambda i: (0, 0))],
        out_specs=pl.BlockSpec((tm, d), lambda i: (i, 0)),
        out_shape=jax.ShapeDtypeStruct((m, d), out_dtype),
        compiler_params=_params(("arbitrary",), 40),
        name="rmsnorm",
    )(x, g.reshape(1, d).astype(F32))


def _mm_kernel(a_ref, b_ref, o_ref):
    o_ref[...] = jnp.dot(a_ref[...], b_ref[...],
                         preferred_element_type=F32).astype(o_ref.dtype)


def _matmul(a, b, out_dtype, tm, tn, name):
    m, k = a.shape
    n = b.shape[1]
    return pl.pallas_call(
        _mm_kernel,
        grid=(m // tm, n // tn),
        in_specs=[pl.BlockSpec((tm, k), lambda i, j: (i, 0)),
                  pl.BlockSpec((k, tn), lambda i, j: (0, j))],
        out_specs=pl.BlockSpec((tm, tn), lambda i, j: (i, j)),
        out_shape=jax.ShapeDtypeStruct((m, n), out_dtype),
        compiler_params=_params(("arbitrary", "arbitrary"), 56),
        name=name,
    )(a, b)


def _krope_kernel(a_ref, b_ref, c_ref, s1_ref, s2_ref, o_ref):
    acc = jnp.dot(a_ref[...], b_ref[...], preferred_element_type=F32)
    o_ref[...] = _rope_slab(acc, c_ref[...], s1_ref[...], s2_ref[...]).astype(o_ref.dtype)


def _krope(a, b, tabs, tm=1024):
    m, k = a.shape
    nt = SEQ // tm
    tab_spec = pl.BlockSpec((tm, LANES), lambda i: (i % nt, 0))
    return pl.pallas_call(
        _krope_kernel,
        grid=(m // tm,),
        in_specs=[pl.BlockSpec((tm, k), lambda i: (i, 0)),
                  pl.BlockSpec((k, LANES), lambda i: (0, 0)),
                  tab_spec, tab_spec, tab_spec],
        out_specs=pl.BlockSpec((tm, LANES), lambda i: (i, 0)),
        out_shape=jax.ShapeDtypeStruct((m, LANES), BF16),
        compiler_params=_params(("arbitrary",), 40),
        name="krope_proj",
    )(a, b, *tabs)


def _qup_kernel(a_ref, g_ref, b_ref, c_ref, s1_ref, s2_ref, o_ref, an_ref, *, heads_per_tile):
    @pl.when(pl.program_id(1) == 0)
    def _():
        an_ref[...] = _rms(a_ref[...], g_ref[...]).astype(BF16)

    acc = jnp.dot(an_ref[...], b_ref[...], preferred_element_type=F32)
    c, s1, s2 = c_ref[...], s1_ref[...], s2_ref[...]
    for h in range(heads_per_tile):
        lo = h * MLA_QK_PAD
        o_ref[:, lo:lo + LANES] = acc[:, lo:lo + LANES].astype(o_ref.dtype)
        o_ref[:, lo + LANES:lo + 2 * LANES] = _rope_slab(
            acc[:, lo + LANES:lo + 2 * LANES], c, s1, s2).astype(o_ref.dtype)


def _qup(lat, g, w, tabs, tm=1024, tn=1024):
    m = lat.shape[0]
    k = MLA_Q_RANK
    n = w.shape[1]
    nt = SEQ // tm
    tab_spec = pl.BlockSpec((tm, LANES), lambda i, j: (i % nt, 0))
    return pl.pallas_call(
        functools.partial(_qup_kernel, heads_per_tile=tn // MLA_QK_PAD),
        grid=(m // tm, n // tn),
        in_specs=[pl.BlockSpec((tm, k), lambda i, j: (i, 0)),
                  pl.BlockSpec((1, k), lambda i, j: (0, 0)),
                  pl.BlockSpec((k, tn), lambda i, j: (0, j)),
                  tab_spec, tab_spec, tab_spec],
        out_specs=pl.BlockSpec((tm, tn), lambda i, j: (i, j)),
        out_shape=jax.ShapeDtypeStruct((m, n), BF16),
        scratch_shapes=[pltpu.VMEM((tm, k), BF16)],
        compiler_params=_params(("arbitrary", "arbitrary"), 48),
        name="mla_q_up",
    )(lat, g.reshape(1, k).astype(F32), w, *tabs)


def _kvup_kernel(a_ref, g_ref, b_ref, kr_ref, k_ref, v_ref):
    xn = _rms(a_ref[...], g_ref[...]).astype(BF16)
    acc = jnp.dot(xn, b_ref[...], preferred_element_type=F32)
    kr = kr_ref[...]
    for h in range(MLA_HEADS):
        lo = h * MLA_QK_PAD
        k_ref[:, lo:lo + LANES] = acc[:, h * LANES:(h + 1) * LANES].astype(k_ref.dtype)
        k_ref[:, lo + LANES:lo + 2 * LANES] = kr
    v_ref[...] = acc[:, MLA_HEADS * MLA_NOPE:].astype(v_ref.dtype)


def _kvup(lat, g, w, krope, tm=512):
    m = lat.shape[0]
    k = MLA_KV_RANK
    n = w.shape[1]
    lat_blk = MLA_Q_RANK // MLA_KV_RANK
    return pl.pallas_call(
        _kvup_kernel,
        grid=(m // tm,),
        in_specs=[pl.BlockSpec((tm, k), lambda i: (i, lat_blk)),
                  pl.BlockSpec((1, k), lambda i: (0, 0)),
                  pl.BlockSpec((k, n), lambda i: (0, 0)),
                  pl.BlockSpec((tm, LANES), lambda i: (i, 0))],
        out_specs=[pl.BlockSpec((tm, MLA_HEADS * MLA_QK_PAD), lambda i: (i, 0)),
                   pl.BlockSpec((tm, MLA_HEADS * MLA_V), lambda i: (i, 0))],
        out_shape=[jax.ShapeDtypeStruct((m, MLA_HEADS * MLA_QK_PAD), BF16),
                   jax.ShapeDtypeStruct((m, MLA_HEADS * MLA_V), BF16)],
        compiler_params=_params(("arbitrary",), 48),
        name="mla_kv_up",
    )(lat, g.reshape(1, k).astype(F32), w, krope)


def _qk(q, k):
    return lax.dot_general(q, k, (((1,), (1,)), ((), ())), preferred_element_type=F32)


def _tile_iotas():
    row = lax.broadcasted_iota(jnp.int32, (ATT_TQ, ATT_TK), 0)
    col = lax.broadcasted_iota(jnp.int32, (ATT_TQ, ATT_TK), 1)
    return row, col


def _chunk_visible(row, col):
    shift = CHUNK.bit_length() - 1
    return jnp.right_shift(col, shift) <= jnp.right_shift(row, shift)


def _mla_attn_kernel(q_ref, k_ref, v_ref, o_ref):
    qi = pl.program_id(2)
    scale = 1.0 / math.sqrt(MLA_NOPE + MLA_ROPE)
    q = q_ref[...]
    row, col = _tile_iotas()
    visible = _chunk_visible(row, col)

    def scores(j):
        start = pl.multiple_of(j * ATT_TK, ATT_TK)
        return (_qk(q, k_ref[pl.ds(start, ATT_TK), :]) * scale,
                v_ref[pl.ds(start, ATT_TK), :])

    s, v = scores(qi)
    s = jnp.where(visible, s, -1e30)
    m = jnp.max(s, axis=-1, keepdims=True)
    p = jnp.exp(s - m)
    l = jnp.sum(p, axis=-1, keepdims=True)
    acc = jnp.dot(p.astype(BF16), v, preferred_element_type=F32)

    def body(j, carry):
        m, l, acc = carry
        s, v = scores(j)
        m_new = jnp.maximum(m, jnp.max(s, axis=-1, keepdims=True))
        alpha = jnp.exp(m - m_new)
        p = jnp.exp(s - m_new)
        l = alpha * l + jnp.sum(p, axis=-1, keepdims=True)
        acc = alpha * acc + jnp.dot(p.astype(BF16), v, preferred_element_type=F32)
        return m_new, l, acc

    m, l, acc = lax.fori_loop(0, qi, body, (m, l, acc))
    o_ref[...] = (acc / l).astype(o_ref.dtype)


def _mla_attention(q, k, v):
    m = q.shape[0]
    batch = m // SEQ
    nq = SEQ // ATT_TQ
    return pl.pallas_call(
        _mla_attn_kernel,
        grid=(batch, MLA_HEADS, nq),
        in_specs=[pl.BlockSpec((ATT_TQ, MLA_QK_PAD), lambda b, h, i: (b * nq + i, h)),
                  pl.BlockSpec((SEQ, MLA_QK_PAD), lambda b, h, i: (b, h)),
                  pl.BlockSpec((SEQ, MLA_V), lambda b, h, i: (b, h))],
        out_specs=pl.BlockSpec((ATT_TQ, MLA_V), lambda b, h, i: (b * nq + i, h)),
        out_shape=jax.ShapeDtypeStruct((m, MLA_HEADS * MLA_V), BF16),
        compiler_params=_params(("arbitrary", "arbitrary", "arbitrary"), 40),
        name="mla_attention",
    )(q, k, v)


def _sb_attn_kernel(q_ref, k_ref, v_ref, o_ref):
    qi = pl.program_id(2)
    scale = 1.0 / math.sqrt(SB_DIM)
    q = q_ref[...]
    row, col = _tile_iotas()
    strict = col < row
    later = (row > col).astype(BF16)

    def block(j, r, acc, diagonal):
        start = pl.multiple_of(j * ATT_TK, ATT_TK)
        z = _qk(q, k_ref[pl.ds(start, ATT_TK), :]) * scale
        t = jnp.log(1.0 + jnp.exp(-jnp.abs(z)))
        log_beta = jnp.minimum(z, 0.0) - t
        log_keep = jnp.minimum(-z, 0.0) - t
        if diagonal:
            log_keep = jnp.where(strict, log_keep, 0.0)
        hi = log_keep.astype(BF16)
        lo = (log_keep - hi.astype(F32)).astype(BF16)
        after = (jnp.dot(hi, later, preferred_element_type=F32)
                 + jnp.dot(lo, later, preferred_element_type=F32))
        a = jnp.exp(log_beta + after + r)
        if diagonal:
            a = jnp.where(strict, a, 0.0)
        acc = acc + jnp.dot(a.astype(BF16), v_ref[pl.ds(start, ATT_TK), :],
                            preferred_element_type=F32)
        r = r + after[:, 0:1] + log_keep[:, 0:1]
        return r, acc

    r0 = jnp.zeros((ATT_TQ, 1), F32)
    acc0 = jnp.zeros((ATT_TQ, SB_DIM), F32)
    r, acc = block(qi, r0, acc0, True)

    def body(it, carry):
        return block(qi - 1 - it, carry[0], carry[1], False)

    r, acc = lax.fori_loop(0, qi, body, (r, acc))
    o_ref[...] = acc.astype(o_ref.dtype)


def _sb_attention(qkv):
    m = qkv.shape[0]
    batch = m // SEQ
    nq = SEQ // ATT_TQ
    return pl.pallas_call(
        _sb_attn_kernel,
        grid=(batch, SB_HEADS, nq),
        in_specs=[pl.BlockSpec((ATT_TQ, SB_DIM), lambda b, h, i: (b * nq + i, h)),
                  pl.BlockSpec((SEQ, SB_DIM), lambda b, h, i: (b, SB_HEADS + h)),
                  pl.BlockSpec((SEQ, SB_DIM), lambda b, h, i: (b, 2 * SB_HEADS + h))],
        out_specs=pl.BlockSpec((ATT_TQ, SB_DIM), lambda b, h, i: (b * nq + i, h)),
        out_shape=jax.ShapeDtypeStruct((m, SB_HEADS * SB_DIM), BF16),
        compiler_params=_params(("arbitrary", "arbitrary", "arbitrary"), 40),
        name="sb_attention",
    )(qkv, qkv, qkv)


def _diff_attn_kernel(q_ref, k_ref, v_ref, slope_ref, lq1_ref, lk1_ref, lq2_ref, lk2_ref,
                      g_ref, o_ref, *, lambda_init):
    qi = pl.program_id(2)
    scale = 1.0 / math.sqrt(DIFF_DIM)
    d = DIFF_DIM
    q1 = q_ref[:, :d]
    q2 = q_ref[:, d:]
    slope = slope_ref[0, :, 0:1]
    row, col = _tile_iotas()
    visible = _chunk_visible(row, col)
    rel =(row - col).astype(F32)
    bias_diag = -slope * jnp.abs(rel)
    bias_off = -slope * rel

    def qk_pair(j):
        start = pl.multiple_of(j * ATT_TK, ATT_TK)
        k = k_ref[pl.ds(start, ATT_TK), :]
        v = v_ref[pl.ds(start, ATT_TK), :]
        return _qk(q1, k[:, :d]) * scale, _qk(q2, k[:, d:]) * scale, v

    def first(s):
        s = jnp.where(visible, s + bias_diag, -1e30)
        m = jnp.max(s, axis=-1, keepdims=True)
        p = jnp.exp(s - m)
        return m, jnp.sum(p, axis=-1, keepdims=True), p

    s1, s2, v = qk_pair(qi)
    m1, l1, p1 = first(s1)
    m2, l2, p2 = first(s2)
    acc1 = jnp.dot(p1.astype(BF16), v, preferred_element_type=F32)
    acc2 = jnp.dot(p2.astype(BF16), v, preferred_element_type=F32)

    def update(u, off, v, m, l, acc):
        m_new = jnp.maximum(m, jnp.max(u, axis=-1, keepdims=True) + off)
        alpha = jnp.exp(m - m_new)
        p = jnp.exp(u - (m_new - off))
        l = alpha * l + jnp.sum(p, axis=-1, keepdims=True)
        acc = alpha * acc + jnp.dot(p.astype(BF16), v, preferred_element_type=F32)
        return m_new, l, acc

    def body(it, carry):
        m1, l1, acc1, m2, l2, acc2 = carry
        j = qi - 1 - it
        s1, s2, v = qk_pair(j)
        off = -slope * ((qi - j) * ATT_TK).astype(F32)
        m1, l1, acc1 = update(s1 + bias_off, off, v, m1, l1, acc1)
        m2, l2, acc2 = update(s2 + bias_off, off, v, m2, l2, acc2)
        return m1, l1, acc1, m2, l2, acc2

    m1, l1, acc1, m2, l2, acc2 = lax.fori_loop(0, qi, body, (m1, l1, acc1, m2, l2, acc2))

    lam = (jnp.exp(jnp.sum(lq1_ref[...] * lk1_ref[...], axis=-1, keepdims=True))
           - jnp.exp(jnp.sum(lq2_ref[...] * lk2_ref[...], axis=-1, keepdims=True))
           + lambda_init)
    o = acc1 / l1 - lam * (acc2 / l2)
    o_ref[...] = (_rms(o, g_ref[...]) * (1.0 - lambda_init)).astype(o_ref.dtype)


def _diff_attention(qkv, slopes, lq1, lk1, lq2, lk2, subln, lambda_init):
    m = qkv.shape[0]
    batch = m // SEQ
    nq = SEQ // ATT_TQ
    dh = 2 * DIFF_DIM
    vec = pl.BlockSpec((1, DIFF_DIM), lambda b, h, i: (0, 0))
    return pl.pallas_call(
        functools.partial(_diff_attn_kernel, lambda_init=lambda_init),
        grid=(batch, DIFF_HEADS, nq),
        in_specs=[pl.BlockSpec((ATT_TQ, dh), lambda b, h, i: (b * nq + i, h)),
                  pl.BlockSpec((SEQ, dh), lambda b, h, i: (b, DIFF_HEADS + h)),
                  pl.BlockSpec((SEQ, dh), lambda b, h, i: (b, 2 * DIFF_HEADS + h)),
                  pl.BlockSpec((1, 1, LANES), lambda b, h, i: (h, 0, 0)),
                  vec, vec, vec, vec,
                  pl.BlockSpec((1, dh), lambda b, h, i: (0, 0))],
        out_specs=pl.BlockSpec((ATT_TQ, dh), lambda b, h, i: (b * nq + i, h)),
        out_shape=jax.ShapeDtypeStruct((m, DIFF_HEADS * dh), BF16),
        compiler_params=_params(("arbitrary", "arbitrary", "arbitrary"), 40),
        name="diff_attention",
    )(qkv, qkv, qkv, slopes,
      lq1.reshape(1, -1).astype(F32), lk1.reshape(1, -1).astype(F32),
      lq2.reshape(1, -1).astype(F32), lk2.reshape(1, -1).astype(F32),
      subln.reshape(1, dh).astype(F32))


def _out_proj_kernel(a1_ref, a2_ref, b_ref, r_ref, o_ref):
    half = a1_ref.shape[1]
    acc = jnp.dot(a1_ref[...], b_ref[:half, :], preferred_element_type=F32)
    acc = acc + jnp.dot(a2_ref[...], b_ref[half:, :], preferred_element_type=F32)
    o_ref[...] = r_ref[...] + acc


def _out_proj(a1, a2, blk1, blk2, b, res, tm=1024, tn=512):
    m = res.shape[0]
    k, n = b.shape
    half = k // 2
    return pl.pallas_call(
        _out_proj_kernel,
        grid=(m // tm, n // tn),
        in_specs=[pl.BlockSpec((tm, half), lambda i, j: (i, blk1)),
                  pl.BlockSpec((tm, half), lambda i, j: (i, blk2)),
                  pl.BlockSpec((k, tn), lambda i, j: (0, j)),
                  pl.BlockSpec((tm, tn), lambda i, j: (i, j))],
        out_specs=pl.BlockSpec((tm, tn), lambda i, j: (i, j)),
        out_shape=jax.ShapeDtypeStruct((m, n), F32),
        compiler_params=_params(("arbitrary", "arbitrary"), 48),
        name="out_proj",
    )(a1, a2, b, res)


def _ffn_up_kernel(a_ref, b_ref, cw_ref, cb_ref, o_ref, h_ref, carry_ref, *, tm, tiles_per_seq):
    i = pl.program_id(0)
    j = pl.program_id(1)
    halo = 8
    width = 2 * FFN_TN

    @pl.when(i % tiles_per_seq == 0)
    def _():
        h_ref[0:halo, :] = jnp.zeros((halo, width), F32)

    @pl.when(i % tiles_per_seq != 0)
    def _():
        h_ref[0:halo, :] = carry_ref[j]

    h_ref[halo:halo + tm, :] = jnp.dot(a_ref[...], b_ref[...], preferred_element_type=F32)
    carry_ref[j] = h_ref[tm:tm + halo, :]

    w0 = cw_ref[0, 0:1, :]
    w1 = cw_ref[0, 1:2, :]
    w2 = cw_ref[0, 2:3, :]
    bias = cb_ref[0]
    rows = 128
    for c in range(tm // rows):
        base = halo + c * rows
        conv = (h_ref[base - 2:base - 2 + rows, :] * w0
                + h_ref[base - 1:base - 1 + rows, :] * w1
                + h_ref[base:base + rows, :] * w2 + bias)
        g = conv[:, :FFN_TN]
        u = conv[:, FFN_TN:]
        o_ref[c * rows:(c + 1) * rows, :] = (g * jax.nn.sigmoid(g) * u).astype(o_ref.dtype)


def _ffn_up(a, b, conv_w, conv_b, tm=1024):
    m, k = a.shape
    nj = FFN_DIM // FFN_TN
    width = 2 * FFN_TN
    return pl.pallas_call(
        functools.partial(_ffn_up_kernel, tm=tm, tiles_per_seq=SEQ // tm),
        grid=(m // tm, nj),
        in_specs=[pl.BlockSpec((tm, k), lambda i, j: (i, 0)),
                  pl.BlockSpec((k, width), lambda i, j: (0, j)),
                  pl.BlockSpec((1, CONV_W, width), lambda i, j: (j, 0, 0)),
                  pl.BlockSpec((1, 1, width), lambda i, j: (j, 0, 0))],
        out_specs=pl.BlockSpec((tm, FFN_TN), lambda i, j: (i, j)),
        out_shape=jax.ShapeDtypeStruct((m, FFN_DIM), BF16),
        scratch_shapes=[pltpu.VMEM((tm + 8, width), F32),
                        pltpu.VMEM((nj, 8, width), F32)],
        compiler_params=_params(("arbitrary", "arbitrary"), 48),
        name="ffn_up_conv_gate",
    )(a, b, conv_w, conv_b)


def _ffn_down_kernel(a_ref, b_ref, r_ref, o_ref):
    o_ref[...] = r_ref[...] + jnp.dot(a_ref[...], b_ref[...], preferred_element_type=F32)


def _ffn_down(a, b, res, tm=512, tn=256):
    m, k = a.shape
    n = b.shape[1]
    return pl.pallas_call(
        _ffn_down_kernel,
        grid=(m // tm, n // tn),
        in_specs=[pl.BlockSpec((tm, k), lambda i, j: (i, 0)),
                  pl.BlockSpec((k, tn), lambda i, j: (0, j)),
                  pl.BlockSpec((tm, tn), lambda i, j: (i, j))],
        out_specs=pl.BlockSpec((tm, tn), lambda i, j: (i, j)),
        out_shape=jax.ShapeDtypeStruct((m, n), F32),
        compiler_params=_params(("arbitrary", "arbitrary"), 48),
        name="ffn_down",
    )(a, b, res)


def _rope_tables():
    half = MLA_ROPE // 2
    inv = ROPE_THETA ** (-jnp.arange(half, dtype=F32) / half)
    ang = jnp.arange(SEQ).astype(F32)[:, None] * inv[None, :]
    cos, sin = jnp.cos(ang), jnp.sin(ang)
    zero = jnp.zeros_like(cos)
    pad = jnp.zeros((SEQ, LANES - MLA_ROPE), F32)
    c = jnp.concatenate([cos, cos, pad], axis=1)
    s1 = jnp.concatenate([-sin, zero, pad], axis=1)
    s2 = jnp.concatenate([zero, sin, pad], axis=1)
    return c, s1, s2


def _interleave_gate_up(w):
    lead = w.shape[:-1]
    nj = FFN_DIM // FFN_TN
    w = w.reshape(*lead, 2, nj, FFN_TN)
    return jnp.swapaxes(w, -3, -2).reshape(*lead, 2 * FFN_DIM)


def _conv_ffn(h, hn, w_up, conv_w, conv_b, w_down):
    nj = FFN_DIM // FFN_TN
    width = 2 * FFN_TN
    w_up = _interleave_gate_up(w_up).astype(BF16)
    cw = _interleave_gate_up(conv_w.astype(F32)).reshape(CONV_W, nj, width).transpose(1, 0, 2)
    cb = _interleave_gate_up(conv_b.astype(F32)).reshape(nj, 1, width)
    act = _ffn_up(hn, w_up, cw, cb)
    return _ffn_down(act, w_down.astype(BF16), h)


def _mla_sb_layer(h, hn, w_in, q_norm, w_uq, kv_norm, w_ukv, w_out, tabs):
    o1 = MLA_Q_RANK + MLA_KV_RANK
    o2 = o1 + MLA_ROPE
    w_lat = w_in[:, :o1].astype(BF16)
    w_kr = jnp.pad(w_in[:, o1:o2], ((0, 0), (0, LANES - MLA_ROPE))).astype(BF16)
    w_sb = w_in[:, o2:].astype(BF16)
    w_q = jnp.pad(w_uq.reshape(MLA_Q_RANK, MLA_HEADS, MLA_NOPE + MLA_ROPE),
                  ((0, 0), (0, 0), (0, MLA_QK_PAD - MLA_NOPE - MLA_ROPE)))
    w_q = w_q.reshape(MLA_Q_RANK, MLA_HEADS * MLA_QK_PAD).astype(BF16)
    w_kv = w_ukv.reshape(MLA_KV_RANK, MLA_HEADS, 2, MLA_NOPE).transpose(0, 2, 1, 3)
    w_kv = w_kv.reshape(MLA_KV_RANK, 2 * MLA_HEADS * MLA_NOPE).astype(BF16)

    lat = _matmul(hn, w_lat, F32, 1024, 768, "mla_latent_proj")
    krope = _krope(hn, w_kr, tabs)
    sb = _matmul(hn, w_sb, BF16, 1024, 1024, "sb_qkv_proj")
    q = _qup(lat, q_norm, w_q, tabs)
    k, v = _kvup(lat, kv_norm, w_kv, krope)
    o_a = _mla_attention(q, k, v)
    o_b = _sb_attention(sb)
    return _out_proj(o_a, o_b, 0, 0, w_out.astype(BF16), h)


def _diff_layer(h, hn, w_in, lq1, lk1, lq2, lk2, subln, w_out, layer_idx):
    lambda_init = 0.8 - 0.6 * math.exp(-0.3 * layer_idx)
    slopes = jnp.asarray([2.0 ** (-8.0 * (i + 1) / DIFF_HEADS) for i in range(DIFF_HEADS)], F32)
    slopes = jnp.broadcast_to(slopes[:, None, None], (DIFF_HEADS, 1, LANES))
    qkv = _matmul(hn, w_in.astype(BF16), BF16, 1024, 1024, "diff_qkv_proj")
    o = _diff_attention(qkv, slopes, lq1, lk1, lq2, lk2, subln, lambda_init)
    return _out_proj(o, o, 0, 1, w_out.astype(BF16), h)


def kernel(x, norm_mix, norm_ffn, norm_final, ab_w_in, ab_q_norm, ab_w_uq, ab_kv_norm,
           ab_w_ukv, ab_w_out, c_w_in, c_lambda_q1, c_lambda_k1, c_lambda_q2, c_lambda_k2,
           c_subln, c_w_out, ffn_w_up, ffn_conv_w, ffn_conv_b, ffn_w_down):
    batch, seq, d = x.shape
    depth = norm_mix.shape[0]
    tabs = _rope_tables()
    h = x.reshape(batch * seq, d)
    for i in range(depth):
        j = i // 2
        hn = _rmsnorm(h, norm_mix[i], BF16)
        if i % 2 == 0:
            h = _mla_sb_layer(h, hn, ab_w_in[j], ab_q_norm[j], ab_w_uq[j], ab_kv_norm[j],
                              ab_w_ukv[j], ab_w_out[j], tabs)
        else:
            h = _diff_layer(h, hn, c_w_in[j], c_lambda_q1[j], c_lambda_k1[j], c_lambda_q2[j],
                            c_lambda_k2[j], c_subln[j], c_w_out[j], i)
        hn = _rmsnorm(h, norm_ffn[i], BF16)
        h = _conv_ffn(h, hn, ffn_w_up[i], ffn_conv_w[i], ffn_conv_b[i], ffn_w_down[i])
    return _rmsnorm(h, norm_final, x.dtype).reshape(batch, seq, d)
```

```python
import functools
import math

import jax
import jax.numpy as jnp
from jax import lax
from jax.experimental import pallas as pl
from jax.experimental.pallas import tpu as pltpu

F32 = jnp.float32
BF16 = jnp.bfloat16

CHUNK = 64
EPS = 1e-6
ROPE_THETA = 10000.0
MLA_HEADS = 16
MLA_Q_RANK = 1024
MLA_KV_RANK = 512
MLA_NOPE = 128
MLA_ROPE = 64
MLA_V = 128
SB_HEADS = 16
SB_DIM = 128
DIFF_HEADS = 16
DIFF_DIM = 128
FFN_DIM = 11008
CONV_W = 3

LANES = 128
MLA_QK_PAD = 256
ATT_TQ = 256
ATT_TK = 256
FFN_TN = 256
FFN_HALO = 8
MIB = 1024 * 1024


def _params(sem, vmem_mib):
    return pltpu.CompilerParams(dimension_semantics=sem, vmem_limit_bytes=vmem_mib * MIB)


def _rms(x, g):
    return x * lax.rsqrt(jnp.mean(x * x, axis=-1, keepdims=True) + EPS) * g


def _rope_slab(x, c, s1, s2):
    return x * c + pltpu.roll(x, 96, 1) * s1 + pltpu.roll(x, 32, 1) * s2


def _rmsnorm_kernel(x_ref, g_ref, o_ref):
    o_ref[...] = _rms(x_ref[...].astype(F32), g_ref[...]).astype(o_ref.dtype)


def _rmsnorm(x, g, out_dtype, tm=256):
    m, d = x.shape
    return pl.pallas_call(
        _rmsnorm_kernel,
        grid=(m // tm,),
        in_specs=[pl.BlockSpec((tm, d), lambda i: (i, 0)),
                  pl.BlockSpec((1, d), lambda i: (0, 0))],
        out_specs=pl.BlockSpec((tm, d), lambda i: (i, 0)),
        out_shape=jax.ShapeDtypeStruct((m, d), out_dtype),
        compiler_params=_params(("arbitrary",), 40),
        name="rmsnorm",
    )(x, g.reshape(1, d).astype(F32))


def _mm_kernel(a_ref, b_ref, o_ref):
    o_ref[...] = jnp.dot(a_ref[...], b_ref[...],
                         preferred_element_type=F32).astype(o_ref.dtype)


def _matmul(a, b, out_dtype, tm, tn, name):
    m, k = a.shape
    n = b.shape[1]
    return pl.pallas_call(
        _mm_kernel,
        grid=(m // tm, n // tn),
        in_specs=[pl.BlockSpec((tm, k), lambda i, j: (i, 0)),
                  pl.BlockSpec((k, tn), lambda i, j: (0, j))],
        out_specs=pl.BlockSpec((tm, tn), lambda i, j: (i, j)),
        out_shape=jax.ShapeDtypeStruct((m, n), out_dtype),
        compiler_params=_params(("arbitrary", "arbitrary"), 56),
        name=name,
    )(a, b)


def _krope_kernel(a_ref, b_ref, c_ref, s1_ref, s2_ref, o_ref):
    acc = jnp.dot(a_ref[...], b_ref[...], preferred_element_type=F32)
    o_ref[...] = _rope_slab(acc, c_ref[...], s1_ref[...], s2_ref[...]).astype(o_ref.dtype)


def _krope(a, b, tabs, tm=1024):
    m, k = a.shape
    nt = tabs[0].shape[0] // tm
    tab_spec = pl.BlockSpec((tm, LANES), lambda i: (i % nt, 0))
    return pl.pallas_call(
        _krope_kernel,
        grid=(m // tm,),
        in_specs=[pl.BlockSpec((tm, k), lambda i: (i, 0)),
                  pl.BlockSpec((k, LANES), lambda i: (0, 0)),
                  tab_spec, tab_spec, tab_spec],
        out_specs=pl.BlockSpec((tm, LANES), lambda i: (i, 0)),
        out_shape=jax.ShapeDtypeStruct((m, LANES), BF16),
        compiler_params=_params(("arbitrary",), 40),
        name="krope_proj",
    )(a, b, *tabs)


def _qup_kernel(a_ref, g_ref, b_ref, c_ref, s1_ref, s2_ref, o_ref, an_ref, *, heads_per_tile):
    @pl.when(pl.program_id(1) == 0)
    def _():
        an_ref[...] = _rms(a_ref[...], g_ref[...]).astype(BF16)

    acc = jnp.dot(an_ref[...], b_ref[...], preferred_element_type=F32)
    c, s1, s2 = c_ref[...], s1_ref[...], s2_ref[...]
    for h in range(heads_per_tile):
        lo = h * MLA_QK_PAD
        o_ref[:, lo:lo + LANES] = acc[:, lo:lo + LANES].astype(o_ref.dtype)
        o_ref[:, lo + LANES:lo + 2 * LANES] = _rope_slab(
            acc[:, lo + LANES:lo + 2 * LANES], c, s1, s2).astype(o_ref.dtype)


def _qup(lat, g, w, tabs, tm=1024, tn=1024):
    m = lat.shape[0]
    k = MLA_Q_RANK
    n = w.shape[1]
    nt = tabs[0].shape[0] // tm
    tab_spec = pl.BlockSpec((tm, LANES), lambda i, j: (i % nt, 0))
    return pl.pallas_call(
        functools.partial(_qup_kernel, heads_per_tile=tn // MLA_QK_PAD),
        grid=(m // tm, n // tn),
        in_specs=[pl.BlockSpec((tm, k), lambda i, j: (i, 0)),
                  pl.BlockSpec((1, k), lambda i, j: (0, 0)),
                  pl.BlockSpec((k, tn), lambda i, j: (0, j)),
                  tab_spec, tab_spec, tab_spec],
        out_specs=pl.BlockSpec((tm, tn), lambda i, j: (i, j)),
        out_shape=jax.ShapeDtypeStruct((m, n), BF16),
        scratch_shapes=[pltpu.VMEM((tm, k), BF16)],
        compiler_params=_params(("arbitrary", "arbitrary"), 48),
        name="mla_q_up",
    )(lat, g.reshape(1, k).astype(F32), w, *tabs)


def _kvup_kernel(a_ref, g_ref, b_ref, kr_ref, k_ref, v_ref):
    xn = _rms(a_ref[...], g_ref[...]).astype(BF16)
    acc = jnp.dot(xn, b_ref[...], preferred_element_type=F32)
    kr = kr_ref[...]
    for h in range(MLA_HEADS):
        lo = h * MLA_QK_PAD
        k_ref[:, lo:lo + LANES] = acc[:, h * LANES:(h + 1) * LANES].astype(k_ref.dtype)
        k_ref[:, lo + LANES:lo + 2 * LANES] = kr
    v_ref[...] = acc[:, MLA_HEADS * MLA_NOPE:].astype(v_ref.dtype)


def _kvup(lat, g, w, krope, tm=512):
    m = lat.shape[0]
    k = MLA_KV_RANK
    n = w.shape[1]
    lat_blk = MLA_Q_RANK // MLA_KV_RANK
    return pl.pallas_call(
        _kvup_kernel,
        grid=(m // tm,),
        in_specs=[pl.BlockSpec((tm, k), lambda i: (i, lat_blk)),
                  pl.BlockSpec((1, k), lambda i: (0, 0)),
                  pl.BlockSpec((k, n), lambda i: (0, 0)),
                  pl.BlockSpec((tm, LANES), lambda i: (i, 0))],
        out_specs=[pl.BlockSpec((tm, MLA_HEADS * MLA_QK_PAD), lambda i: (i, 0)),
                   pl.BlockSpec((tm, MLA_HEADS * MLA_V), lambda i: (i, 0))],
        out_shape=[jax.ShapeDtypeStruct((m, MLA_HEADS * MLA_QK_PAD), BF16),
                   jax.ShapeDtypeStruct((m, MLA_HEADS * MLA_V), BF16)],
        compiler_params=_params(("arbitrary",), 48),
        name="mla_kv_up",
    )(lat, g.reshape(1, k).astype(F32), w, krope)


def _tile_iotas():
    key = lax.broadcasted_iota(jnp.int32, (ATT_TK, ATT_TQ), 0)
    qry = lax.broadcasted_iota(jnp.int32, (ATT_TK, ATT_TQ), 1)
    return key, qry


def _chunk_visible(qry, key):
    shift = CHUNK.bit_length() - 1
    return jnp.right_shift(key, shift) <= jnp.right_shift(qry, shift)


def _stage_vt(v_ref, vt_ref, hp, dv):
    for h in range(hp):
        for c in range(v_ref.shape[0] // ATT_TK):
            vt_ref[h, c] = v_ref[c * ATT_TK:(c + 1) * ATT_TK, h * dv:(h + 1) * dv].T


def _key_block(k_ref, j, lo, width):
    start = pl.multiple_of(j * ATT_TK, ATT_TK)
    return k_ref[pl.ds(start, ATT_TK), lo:lo + width]


def _dot(a, b):
    return jnp.dot(a, b, preferred_element_type=F32)


def _colmax(x):
    return jnp.max(x, axis=0, keepdims=True)


def _colsum(x):
    return jnp.sum(x, axis=0, keepdims=True)


def _attn_call(kernel_fn, inputs, in_specs, d_out, dv, n_acc, hp, seq, heads, name):
    m = inputs[0].shape[0]
    nq = seq // ATT_TQ
    return pl.pallas_call(
        kernel_fn,
        grid=(m // seq, heads // hp, nq),
        in_specs=in_specs,
        out_specs=pl.BlockSpec((ATT_TQ, hp * d_out), lambda b, h, i: (b * nq + i, h)),
        out_shape=jax.ShapeDtypeStruct((m, heads * d_out), BF16),
        scratch_shapes=[pltpu.VMEM((hp, seq // ATT_TK, dv, ATT_TK), BF16),
                        pltpu.VMEM((n_acc * hp, dv, ATT_TQ), F32)],
        compiler_params=_params(("arbitrary", "arbitrary", "arbitrary"), 48),
        name=name,
    )(*inputs)


def _mla_attn_kernel(q_ref, k_ref, v_ref, o_ref, vt_ref, acc_ref, *, hp):
    qi = pl.program_id(2)
    dq, dv = MLA_QK_PAD, MLA_V
    heads = range(hp)

    @pl.when(qi == 0)
    def _():
        _stage_vt(v_ref, vt_ref, hp, dv)

    scale = 1.0 / math.sqrt(MLA_NOPE + MLA_ROPE)
    qt = [q_ref[:, h * dq:(h + 1) * dq].T for h in heads]
    key, qry = _tile_iotas()
    visible = _chunk_visible(qry, key)

    def scores(j):
        return [_dot(_key_block(k_ref, j, h * dq, dq), qt[h]) * scale for h in heads]

    s = [jnp.where(visible, x, -1e30) for x in scores(qi)]
    m = [_colmax(x) for x in s]
    p = [jnp.exp(s[h] - m[h]) for h in heads]
    l = [_colsum(x) for x in p]
    for h in heads:
        acc_ref[h] = _dot(vt_ref[h, qi], p[h].astype(BF16))

    def body(j, carry):
        m, l = carry[:hp], carry[hp:]
        s = scores(j)
        m_new = [jnp.maximum(m[h], _colmax(s[h])) for h in heads]
        alpha = [jnp.exp(m[h] - m_new[h]) for h in heads]
        p = [jnp.exp(s[h] - m_new[h]) for h in heads]
        l = [alpha[h] * l[h] + _colsum(p[h]) for h in heads]
        pv = [_dot(vt_ref[h, j], p[h].astype(BF16)) for h in heads]
        for h in heads:
            acc_ref[h] = alpha[h] * acc_ref[h] + pv[h]
        return tuple(m_new) + tuple(l)

    carry = lax.fori_loop(0, qi, body, tuple(m) + tuple(l))
    l = carry[hp:]
    for h in heads:
        o_ref[:, h * dv:(h + 1) * dv] = (acc_ref[h] / l[h]).T.astype(o_ref.dtype)


def _mla_attention(q, k, v, seq, hp=4):
    nq = seq // ATT_TQ
    in_specs = [pl.BlockSpec((ATT_TQ, hp * MLA_QK_PAD), lambda b, h, i: (b * nq + i, h)),
                pl.BlockSpec((seq, hp * MLA_QK_PAD), lambda b, h, i: (b, h)),
                pl.BlockSpec((seq, hp * MLA_V), lambda b, h, i: (b, h))]
    return _attn_call(functools.partial(_mla_attn_kernel, hp=hp), (q, k, v), in_specs,
                      MLA_V, MLA_V, 1, hp, seq, MLA_HEADS, "mla_attention")


def _sb_attn_kernel(q_ref, k_ref, v_ref, o_ref, vt_ref, acc_ref, *, hp):
    qi = pl.program_id(2)
    d = SB_DIM
    heads = range(hp)

    @pl.when(qi == 0)
    def _():
        _stage_vt(v_ref, vt_ref, hp, d)

    scale = 1.0 / math.sqrt(SB_DIM)
    qt = [q_ref[:, h * d:(h + 1) * d].T for h in heads]
    key, qry = _tile_iotas()
    strict = key < qry
    later = (qry > key).astype(BF16)

    def block(j, r, diagonal):
        z = [_dot(_key_block(k_ref, j, h * d, d), qt[h]) * scale for h in heads]
        t = [jnp.log(1.0 + jnp.exp(-jnp.abs(x))) for x in z]
        log_beta = [jnp.minimum(z[h], 0.0) - t[h] for h in heads]
        log_keep = [jnp.minimum(-z[h], 0.0) - t[h] for h in heads]
        if diagonal:
            log_keep = [jnp.where(strict, x, 0.0) for x in log_keep]
        hi = [x.astype(BF16) for x in log_keep]
        lo = [(log_keep[h] - hi[h].astype(F32)).astype(BF16) for h in heads]
        after = [_dot(later, hi[h]) + _dot(later, lo[h]) for h in heads]
        a = [jnp.exp(log_beta[h] + after[h] + r[h]) for h in heads]
        if diagonal:
            a = [jnp.where(strict, x, 0.0) for x in a]
        pv = [_dot(vt_ref[h, j], a[h].astype(BF16)) for h in heads]
        r = [r[h] + after[h][0:1, :] + log_keep[h][0:1, :] for h in heads]
        return r, pv

    r, pv = block(qi, [jnp.zeros((1, ATT_TQ), F32)] * hp, True)
    for h in heads:
        acc_ref[h] = pv[h]

    def body(it, r):
        r, pv = block(qi - 1 - it, list(r), False)
        for h in heads:
            acc_ref[h] += pv[h]
        return tuple(r)

    lax.fori_loop(0, qi, body, tuple(r))
    for h in heads:
        o_ref[:, h * d:(h + 1) * d] = acc_ref[h].T.astype(o_ref.dtype)


def _sb_attention(qkv, seq, hp=4):
    nq = seq // ATT_TQ
    groups = SB_HEADS // hp
    in_specs = [pl.BlockSpec((ATT_TQ, hp * SB_DIM), lambda b, h, i: (b * nq + i, h)),
                pl.BlockSpec((seq, hp * SB_DIM), lambda b, h, i: (b, groups + h)),
                pl.BlockSpec((seq, hp * SB_DIM), lambda b, h, i: (b, 2 * groups + h))]
    return _attn_call(functools.partial(_sb_attn_kernel, hp=hp), (qkv, qkv, qkv), in_specs,
                      SB_DIM, SB_DIM, 1, hp, seq, SB_HEADS, "sb_attention")


def _diff_attn_kernel(q_ref, k_ref, v_ref, slope_ref, lq1_ref, lk1_ref, lq2_ref, lk2_ref,
                      g_ref, o_ref, vt_ref, acc_ref, *, lambda_init, hp):
    qi = pl.program_id(2)
    d = DIFF_DIM
    dv = 2 * d
    maps = range(2 * hp)

    @pl.when(qi == 0)
    def _():
        _stage_vt(v_ref, vt_ref, hp, dv)

    scale = 1.0 / math.sqrt(DIFF_DIM)
    qt = [q_ref[:, c * d:(c + 1) * d].T for c in maps]
    slope = [slope_ref[c // 2, :, 0:1] for c in maps]
    key, qry = _tile_iotas()
    rel = (qry - key).astype(F32)
    visible = _chunk_visible(qry, key)
    bias_off = [-slope[2 * h] * rel for h in range(hp)]

    def scores(j):
        return [_dot(_key_block(k_ref, j, c * d, d), qt[c]) * scale for c in maps]

    s = scores(qi)
    s = [jnp.where(visible, s[c] - slope[c] * jnp.abs(rel), -1e30) for c in maps]
    m = [_colmax(x) for x in s]
    p = [jnp.exp(s[c] - m[c]) for c in maps]
    l = [_colsum(x) for x in p]
    for c in maps:
        acc_ref[c] = _dot(vt_ref[c // 2, qi], p[c].astype(BF16))

    def body(it, carry):
        m, l = carry[:2 * hp], carry[2 * hp:]
        j = qi - 1 - it
        dist = ((qi - j) * ATT_TK).astype(F32)
        s = scores(j)
        u = [s[c] + bias_off[c // 2] for c in maps]
        off = [-slope[c] * dist for c in maps]
        m_new = [jnp.maximum(m[c], _colmax(u[c]) + off[c]) for c in maps]
        alpha = [jnp.exp(m[c] - m_new[c]) for c in maps]
        p = [jnp.exp(u[c] - (m_new[c] - off[c])) for c in maps]
        l = [alpha[c] * l[c] + _colsum(p[c]) for c in maps]
        pv = [_dot(vt_ref[c // 2, j], p[c].astype(BF16)) for c in maps]
        for c in maps:
            acc_ref[c] = alpha[c] * acc_ref[c] + pv[c]
        return tuple(m_new) + tuple(l)

    carry = lax.fori_loop(0, qi, body, tuple(m) + tuple(l))
    l = carry[2 * hp:]

    lam = (jnp.exp(jnp.sum(lq1_ref[...] * lk1_ref[...], axis=-1, keepdims=True))
           - jnp.exp(jnp.sum(lq2_ref[...] * lk2_ref[...], axis=-1, keepdims=True))
           + lambda_init)
    for h in range(hp):
        ot = acc_ref[2 * h] / l[2 * h] - lam * (acc_ref[2 * h + 1] / l[2 * h + 1])
        yt = ot * lax.rsqrt(jnp.mean(ot * ot, axis=0, keepdims=True) + EPS)
        o_ref[:, h * dv:(h + 1) * dv] = (yt.T * g_ref[...] * (1.0 - lambda_init)).astype(o_ref.dtype)


def _diff_attention(qkv, slopes, lq1, lk1, lq2, lk2, subln, lambda_init, seq, hp=2):
    nq = seq // ATT_TQ
    dh = 2 * DIFF_DIM
    groups = DIFF_HEADS // hp
    vec = pl.BlockSpec((1, DIFF_DIM), lambda b, h, i: (0, 0))
    in_specs = [pl.BlockSpec((ATT_TQ, hp * dh), lambda b, h, i: (b * nq + i, h)),
                pl.BlockSpec((seq, hp * dh), lambda b, h, i: (b, groups + h)),
                pl.BlockSpec((seq, hp * dh), lambda b, h, i: (b, 2 * groups + h)),
                pl.BlockSpec((hp, 1, LANES), lambda b, h, i: (h, 0, 0)),
                vec, vec, vec, vec,
                pl.BlockSpec((1, dh), lambda b, h, i: (0, 0))]
    inputs = (qkv, qkv, qkv, slopes,
              lq1.reshape(1, -1).astype(F32), lk1.reshape(1, -1).astype(F32),
              lq2.reshape(1, -1).astype(F32), lk2.reshape(1, -1).astype(F32),
              subln.reshape(1, dh).astype(F32))
    return _attn_call(functools.partial(_diff_attn_kernel, lambda_init=lambda_init, hp=hp),
                      inputs, in_specs, dh, dh, 2, hp, seq, DIFF_HEADS, "diff_attention")


def _out_proj_kernel(a1_ref, a2_ref, b_ref, r_ref, o_ref):
    half = a1_ref.shape[1]
    acc = jnp.dot(a1_ref[...], b_ref[:half, :], preferred_element_type=F32)
    acc = acc + jnp.dot(a2_ref[...], b_ref[half:, :], preferred_element_type=F32)
    o_ref[...] = r_ref[...] + acc


def _out_proj(a1, a2, blk1, blk2, b, res, tm=1024, tn=512):
    m = res.shape[0]
    k, n = b.shape
    half = k // 2
    return pl.pallas_call(
        _out_proj_kernel,
        grid=(m // tm, n // tn),
        in_specs=[pl.BlockSpec((tm, half), lambda i, j: (i, blk1)),
                  pl.BlockSpec((tm, half), lambda i, j: (i, blk2)),
                  pl.BlockSpec((k, tn), lambda i, j: (0, j)),
                  pl.BlockSpec((tm, tn), lambda i, j: (i, j))],
        out_specs=pl.BlockSpec((tm, tn), lambda i, j: (i, j)),
        out_shape=jax.ShapeDtypeStruct((m, n), F32),
        compiler_params=_params(("arbitrary", "arbitrary"), 48),
        name="out_proj",
    )(a1, a2, b, res)


def _ffn_up_kernel(a_ref, bg_ref, bu_ref, cwg_ref, cwu_ref, cbg_ref, cbu_ref, o_ref,
                   h_ref, carry_ref, *, tm, tiles_per_seq):
    i = pl.program_id(0)
    j = pl.program_id(1)
    halo = FFN_HALO
    width = 2 * FFN_TN

    @pl.when(i % tiles_per_seq == 0)
    def _():
        h_ref[0:halo, :] = jnp.zeros((halo, width), F32)

    @pl.when(i % tiles_per_seq != 0)
    def _():
        h_ref[0:halo, :] = carry_ref[j]

    a = a_ref[...]
    h_ref[halo:halo + tm, 0:FFN_TN] = jnp.dot(a, bg_ref[...], preferred_element_type=F32)
    h_ref[halo:halo + tm, FFN_TN:width] = jnp.dot(a, bu_ref[...], preferred_element_type=F32)
    carry_ref[j] = h_ref[tm:tm + halo, :]

    wg, wu = cwg_ref[...], cwu_ref[...]
    w = [jnp.concatenate([wg[t:t + 1], wu[t:t + 1]], axis=1) for t in range(CONV_W)]
    bias = jnp.concatenate([cbg_ref[...], cbu_ref[...]], axis=1)
    rows = 128
    for c in range(tm // rows):
        base = halo + c * rows
        conv = bias
        for t in range(CONV_W):
            lo = base - (CONV_W - 1 - t)
            conv = conv + h_ref[lo:lo + rows, :] * w[t]
        g = conv[:, :FFN_TN]
        u = conv[:, FFN_TN:]
        o_ref[c * rows:(c + 1) * rows, :] = (g * jax.nn.sigmoid(g) * u).astype(o_ref.dtype)


def _ffn_up(a, w_up, conv_w, conv_b, seq, tm=1024):
    m, k = a.shape
    nj = FFN_DIM // FFN_TN
    width = 2 * FFN_TN
    return pl.pallas_call(
        functools.partial(_ffn_up_kernel, tm=tm, tiles_per_seq=seq // tm),
        grid=(m // tm, nj),
        in_specs=[pl.BlockSpec((tm, k), lambda i, j: (i, 0)),
                  pl.BlockSpec((k, FFN_TN), lambda i, j: (0, j)),
                  pl.BlockSpec((k, FFN_TN), lambda i, j: (0, nj + j)),
                  pl.BlockSpec((CONV_W, FFN_TN), lambda i, j: (0, j)),
                  pl.BlockSpec((CONV_W, FFN_TN), lambda i, j: (0, nj + j)),
                  pl.BlockSpec((1, FFN_TN), lambda i, j: (0, j)),
                  pl.BlockSpec((1, FFN_TN), lambda i, j: (0, nj + j))],
        out_specs=pl.BlockSpec((tm, FFN_TN), lambda i, j: (i, j)),
        out_shape=jax.ShapeDtypeStruct((m, FFN_DIM), BF16),
        scratch_shapes=[pltpu.VMEM((tm + FFN_HALO, width), F32),
                        pltpu.VMEM((nj, FFN_HALO, width), F32)],
        compiler_params=_params(("arbitrary", "arbitrary"), 48),
        name="ffn_up_conv_gate",
    )(a, w_up, w_up, conv_w, conv_w, conv_b, conv_b)


def _ffn_down_kernel(a_ref, b_ref, r_ref, o_ref):
    o_ref[...] = r_ref[...] + jnp.dot(a_ref[...], b_ref[...], preferred_element_type=F32)


def _ffn_down(a, b, res, tm=512, tn=512):
    m, k = a.shape
    n = b.shape[1]
    return pl.pallas_call(
        _ffn_down_kernel,
        grid=(m // tm, n // tn),
        in_specs=[pl.BlockSpec((tm, k), lambda i, j: (i, 0)),
                  pl.BlockSpec((k, tn), lambda i, j: (0, j)),
                  pl.BlockSpec((tm, tn), lambda i, j: (i, j))],
        out_specs=pl.BlockSpec((tm, tn), lambda i, j: (i, j)),
        out_shape=jax.ShapeDtypeStruct((m, n), F32),
        compiler_params=_params(("arbitrary", "arbitrary"), 56),
        name="ffn_down",
    )(a, b, res)


def _rope_tables(seq):
    half = MLA_ROPE // 2
    inv = ROPE_THETA ** (-jnp.arange(half, dtype=F32) / half)
    ang = jnp.arange(seq).astype(F32)[:, None] * inv[None, :]
    cos, sin = jnp.cos(ang), jnp.sin(ang)
    zero = jnp.zeros_like(cos)
    pad = jnp.zeros((seq, LANES - MLA_ROPE), F32)
    c = jnp.concatenate([cos, cos, pad], axis=1)
    s1 = jnp.concatenate([-sin, zero, pad], axis=1)
    s2 = jnp.concatenate([zero, sin, pad], axis=1)
    return c, s1, s2


def _conv_ffn(h, hn, w_up, conv_w, conv_b, w_down, seq):
    act = _ffn_up(hn, w_up.astype(BF16), conv_w.astype(F32),
                  conv_b.reshape(1, -1).astype(F32), seq)
    return _ffn_down(act, w_down.astype(BF16), h)


def _mla_sb_layer(h, hn, w_in, q_norm, w_uq, kv_norm, w_ukv, w_out, tabs, seq):
    o1 = MLA_Q_RANK + MLA_KV_RANK
    o2 = o1 + MLA_ROPE
    w_lat = w_in[:, :o1].astype(BF16)
    w_kr = jnp.pad(w_in[:, o1:o2], ((0, 0), (0, LANES - MLA_ROPE))).astype(BF16)
    w_sb = w_in[:, o2:].astype(BF16)
    w_q = jnp.pad(w_uq.reshape(MLA_Q_RANK, MLA_HEADS, MLA_NOPE + MLA_ROPE),
                  ((0, 0), (0, 0), (0, MLA_QK_PAD - MLA_NOPE - MLA_ROPE)))
    w_q = w_q.reshape(MLA_Q_RANK, MLA_HEADS * MLA_QK_PAD).astype(BF16)
    w_kv = w_ukv.reshape(MLA_KV_RANK, MLA_HEADS, 2, MLA_NOPE).transpose(0, 2, 1, 3)
    w_kv = w_kv.reshape(MLA_KV_RANK, 2 * MLA_HEADS * MLA_NOPE).astype(BF16)

    lat = _matmul(hn, w_lat, F32, 1024, 768, "mla_latent_proj")
    krope = _krope(hn, w_kr, tabs)
    sb = _matmul(hn, w_sb, BF16, 1024, 1024, "sb_qkv_proj")
    q = _qup(lat, q_norm, w_q, tabs)
    k, v = _kvup(lat, kv_norm, w_kv, krope)
    o_a = _mla_attention(q, k, v, seq)
    o_b = _sb_attention(sb, seq)
    return _out_proj(o_a, o_b, 0, 0, w_out.astype(BF16), h)


def _diff_layer(h, hn, w_in, lq1, lk1, lq2, lk2, subln, w_out, layer_idx, seq):
    lambda_init = 0.8 - 0.6 * math.exp(-0.3 * layer_idx)
    slopes = jnp.asarray([2.0 ** (-8.0 * (i + 1) / DIFF_HEADS) for i in range(DIFF_HEADS)], F32)
    slopes = jnp.broadcast_to(slopes[:, None, None], (DIFF_HEADS, 1, LANES))
    qkv = _matmul(hn, w_in.astype(BF16), BF16, 1024, 1024, "diff_qkv_proj")
    o = _diff_attention(qkv, slopes, lq1, lk1, lq2, lk2, subln, lambda_init, seq)
    return _out_proj(o, o, 0, 1, w_out.astype(BF16), h)


def kernel(x, norm_mix, norm_ffn, norm_final, ab_w_in, ab_q_norm, ab_w_uq, ab_kv_norm,
           ab_w_ukv, ab_w_out, c_w_in, c_lambda_q1, c_lambda_k1, c_lambda_q2, c_lambda_k2,
           c_subln, c_w_out, ffn_w_up, ffn_conv_w, ffn_conv_b, ffn_w_down):
    batch, seq, d = x.shape
    depth = norm_mix.shape[0]
    tabs = _rope_tables(seq)
    h = x.reshape(batch * seq, d)
    for i in range(depth):
        j = i // 2
        hn = _rmsnorm(h, norm_mix[i], BF16)
        if i % 2 == 0:
            h = _mla_sb_layer(h, hn, ab_w_in[j], ab_q_norm[j], ab_w_uq[j], ab_kv_norm[j],
                              ab_w_ukv[j], ab_w_out[j], tabs, seq)
        else:
            h = _diff_layer(h, hn, c_w_in[j], c_lambda_q1[j], c_lambda_k1[j], c_lambda_q2[j],
                            c_lambda_k2[j], c_subln[j], c_w_out[j], i, seq)
        hn = _rmsnorm(h, norm_ffn[i], BF16)
        h = _conv_ffn(h, hn, ffn_w_up[i], ffn_conv_w[i], ffn_conv_b[i], ffn_w_down[i], seq)
    return _rmsnorm(h, norm_final, x.dtype).reshape(batch, seq, d)
```

```python
import functools
import math

import jax
import jax.numpy as jnp
from jax import lax
from jax.experimental import pallas as pl
from jax.experimental.pallas import tpu as pltpu

F32 = jnp.float32
BF16 = jnp.bfloat16

CHUNK = 64
EPS = 1e-6
ROPE_THETA = 10000.0
MLA_HEADS = 16
MLA_Q_RANK = 1024
MLA_KV_RANK = 512
MLA_NOPE = 128
MLA_ROPE = 64
MLA_V = 128
SB_HEADS = 16
SB_DIM = 128
DIFF_HEADS = 16
DIFF_DIM = 128
FFN_DIM = 11008
CONV_W = 3

LANES = 128
MLA_QK_PAD = 256
ATT_TQ = 256
ATT_TK = 256
FFN_TN = 256
FFN_HALO = 8
LOG2E = math.log2(math.e)
MIB = 1024 * 1024


def _params(sem, vmem_mib):
    return pltpu.CompilerParams(dimension_semantics=sem, vmem_limit_bytes=vmem_mib * MIB)


def _rms(x, g):
    return x * lax.rsqrt(jnp.mean(x * x, axis=-1, keepdims=True) + EPS) * g


def _rope_slab(x, c, s1, s2):
    return x * c + pltpu.roll(x, 96, 1) * s1 + pltpu.roll(x, 32, 1) * s2


def _rmsnorm_kernel(x_ref, g_ref, o_ref):
    o_ref[...] = _rms(x_ref[...].astype(F32), g_ref[...]).astype(o_ref.dtype)


def _rmsnorm(x, g, out_dtype, tm=256):
    m, d = x.shape
    return pl.pallas_call(
        _rmsnorm_kernel,
        grid=(m // tm,),
        in_specs=[pl.BlockSpec((tm, d), lambda i: (i, 0)),
                  pl.BlockSpec((1, d), lambda i: (0, 0))],
        out_specs=pl.BlockSpec((tm, d), lambda i: (i, 0)),
        out_shape=jax.ShapeDtypeStruct((m, d), out_dtype),
        compiler_params=_params(("arbitrary",), 40),
        name="rmsnorm",
    )(x, g.reshape(1, d).astype(F32))


def _mm_kernel(a_ref, b_ref, o_ref):
    o_ref[...] = jnp.dot(a_ref[...], b_ref[...],
                         preferred_element_type=F32).astype(o_ref.dtype)


def _matmul(a, b, out_dtype, tm, tn, name):
    m, k = a.shape
    n = b.shape[1]
    return pl.pallas_call(
        _mm_kernel,
        grid=(m // tm, n // tn),
        in_specs=[pl.BlockSpec((tm, k), lambda i, j: (i, 0)),
                  pl.BlockSpec((k, tn), lambda i, j: (0, j))],
        out_specs=pl.BlockSpec((tm, tn), lambda i, j: (i, j)),
        out_shape=jax.ShapeDtypeStruct((m, n), out_dtype),
        compiler_params=_params(("arbitrary", "arbitrary"), 56),
        name=name,
    )(a, b)


def _krope_kernel(a_ref, b_ref, c_ref, s1_ref, s2_ref, o_ref):
    acc = jnp.dot(a_ref[...], b_ref[...], preferred_element_type=F32)
    o_ref[...] = _rope_slab(acc, c_ref[...], s1_ref[...], s2_ref[...]).astype(o_ref.dtype)


def _krope(a, b, tabs, tm=1024):
    m, k = a.shape
    nt = tabs[0].shape[0] // tm
    tab_spec = pl.BlockSpec((tm, LANES), lambda i: (i % nt, 0))
    return pl.pallas_call(
        _krope_kernel,
        grid=(m // tm,),
        in_specs=[pl.BlockSpec((tm, k), lambda i: (i, 0)),
                  pl.BlockSpec((k, LANES), lambda i: (0, 0)),
                  tab_spec, tab_spec, tab_spec],
        out_specs=pl.BlockSpec((tm, LANES), lambda i: (i, 0)),
        out_shape=jax.ShapeDtypeStruct((m, LANES), BF16),
        compiler_params=_params(("arbitrary",), 40),
        name="krope_proj",
    )(a, b, *tabs)


def _qup_kernel(a_ref, g_ref, b_ref, c_ref, s1_ref, s2_ref, o_ref, an_ref, *, heads_per_tile):
    @pl.when(pl.program_id(1) == 0)
    def _():
        an_ref[...] = _rms(a_ref[...], g_ref[...]).astype(BF16)

    acc = jnp.dot(an_ref[...], b_ref[...], preferred_element_type=F32)
    c, s1, s2 = c_ref[...], s1_ref[...], s2_ref[...]
    for h in range(heads_per_tile):
        lo = h * MLA_QK_PAD
        o_ref[:, lo:lo + LANES] = acc[:, lo:lo + LANES].astype(o_ref.dtype)
        o_ref[:, lo + LANES:lo + 2 * LANES] = _rope_slab(
            acc[:, lo + LANES:lo + 2 * LANES], c, s1, s2).astype(o_ref.dtype)


def _qup(lat, g, w, tabs, tm=1024, tn=1024):
    m = lat.shape[0]
    k = MLA_Q_RANK
    n = w.shape[1]
    nt = tabs[0].shape[0] // tm
    tab_spec = pl.BlockSpec((tm, LANES), lambda i, j: (i % nt, 0))
    return pl.pallas_call(
        functools.partial(_qup_kernel, heads_per_tile=tn // MLA_QK_PAD),
        grid=(m // tm, n // tn),
        in_specs=[pl.BlockSpec((tm, k), lambda i, j: (i, 0)),
                  pl.BlockSpec((1, k), lambda i, j: (0, 0)),
                  pl.BlockSpec((k, tn), lambda i, j: (0, j)),
                  tab_spec, tab_spec, tab_spec],
        out_specs=pl.BlockSpec((tm, tn), lambda i, j: (i, j)),
        out_shape=jax.ShapeDtypeStruct((m, n), BF16),
        scratch_shapes=[pltpu.VMEM((tm, k), BF16)],
        compiler_params=_params(("arbitrary", "arbitrary"), 48),
        name="mla_q_up",
    )(lat, g.reshape(1, k).astype(F32), w, *tabs)


def _kvup_kernel(a_ref, g_ref, b_ref, kr_ref, k_ref, v_ref):
    xn = _rms(a_ref[...], g_ref[...]).astype(BF16)
    acc = jnp.dot(xn, b_ref[...], preferred_element_type=F32)
    kr = kr_ref[...]
    for h in range(MLA_HEADS):
        lo = h * MLA_QK_PAD
        k_ref[:, lo:lo + LANES] = acc[:, h * LANES:(h + 1) * LANES].astype(k_ref.dtype)
        k_ref[:, lo + LANES:lo + 2 * LANES] = kr
    v_ref[...] = acc[:, MLA_HEADS * MLA_NOPE:].astype(v_ref.dtype)


def _kvup(lat, g, w, krope, tm=512):
    m = lat.shape[0]
    k = MLA_KV_RANK
    n = w.shape[1]
    lat_blk = MLA_Q_RANK // MLA_KV_RANK
    return pl.pallas_call(
        _kvup_kernel,
        grid=(m // tm,),
        in_specs=[pl.BlockSpec((tm, k), lambda i: (i, lat_blk)),
                  pl.BlockSpec((1, k), lambda i: (0, 0)),
                  pl.BlockSpec((k, n), lambda i: (0, 0)),
                  pl.BlockSpec((tm, LANES), lambda i: (i, 0))],
        out_specs=[pl.BlockSpec((tm, MLA_HEADS * MLA_QK_PAD), lambda i: (i, 0)),
                   pl.BlockSpec((tm, MLA_HEADS * MLA_V), lambda i: (i, 0))],
        out_shape=[jax.ShapeDtypeStruct((m, MLA_HEADS * MLA_QK_PAD), BF16),
                   jax.ShapeDtypeStruct((m, MLA_HEADS * MLA_V), BF16)],
        compiler_params=_params(("arbitrary",), 48),
        name="mla_kv_up",
    )(lat, g.reshape(1, k).astype(F32), w, krope)


def _tile_iotas():
    key = lax.broadcasted_iota(jnp.int32, (ATT_TK, ATT_TQ), 0)
    qry = lax.broadcasted_iota(jnp.int32, (ATT_TK, ATT_TQ), 1)
    return key, qry


def _chunk_visible(qry, key):
    shift = CHUNK.bit_length() - 1
    return jnp.right_shift(key, shift) <= jnp.right_shift(qry, shift)


def _stage_vt(v_ref, vt_ref, hp, dv):
    for h in range(hp):
        for c in range(v_ref.shape[0] // ATT_TK):
            vt_ref[h, c] = v_ref[c * ATT_TK:(c + 1) * ATT_TK, h * dv:(h + 1) * dv].T


def _key_block(k_ref, j, lo, width):
    start = pl.multiple_of(j * ATT_TK, ATT_TK)
    return k_ref[pl.ds(start, ATT_TK), lo:lo + width]


def _dot(a, b):
    return jnp.dot(a, b, preferred_element_type=F32)


def _colmax(x):
    return jnp.max(x, axis=0, keepdims=True)


def _colsum(x):
    return jnp.sum(x, axis=0, keepdims=True)


def _attn_call(kernel_fn, inputs, in_specs, d_out, dv, n_acc, hp, seq, heads, name):
    m = inputs[0].shape[0]
    nq = seq // ATT_TQ
    return pl.pallas_call(
        kernel_fn,
        grid=(m // seq, heads // hp, nq),
        in_specs=in_specs,
        out_specs=pl.BlockSpec((ATT_TQ, hp * d_out), lambda b, h, i: (b * nq + i, h)),
        out_shape=jax.ShapeDtypeStruct((m, heads * d_out), BF16),
        scratch_shapes=[pltpu.VMEM((hp, seq // ATT_TK, dv, ATT_TK), BF16),
                        pltpu.VMEM((n_acc * hp, dv, ATT_TQ), F32)],
        compiler_params=_params(("arbitrary", "arbitrary", "arbitrary"), 56),
        name=name,
    )(*inputs)


def _mla_attn_kernel(q_ref, k_ref, v_ref, o_ref, vt_ref, acc_ref, *, hp):
    qi = pl.program_id(2)
    dq, dv = MLA_QK_PAD, MLA_V
    heads = range(hp)

    @pl.when(qi == 0)
    def _():
        _stage_vt(v_ref, vt_ref, hp, dv)

    c = LOG2E / math.sqrt(MLA_NOPE + MLA_ROPE)
    qt = [q_ref[:, h * dq:(h + 1) * dq].T for h in heads]
    key, qry = _tile_iotas()
    visible = _chunk_visible(qry, key)

    def scores(j):
        return [_dot(_key_block(k_ref, j, h * dq, dq), qt[h]) for h in heads]

    s = [jnp.where(visible, x, -1e30) for x in scores(qi)]
    m = [_colmax(x) for x in s]
    p = [jnp.exp2((s[h] - m[h]) * c) for h in heads]
    l = [_colsum(x) for x in p]
    for h in heads:
        acc_ref[h] = _dot(vt_ref[h, qi], p[h].astype(BF16))

    def body(j, carry):
        m, l = carry[:hp], carry[hp:]
        s = scores(j)
        m_new = [jnp.maximum(m[h], _colmax(s[h])) for h in heads]
        alpha = [jnp.exp2((m[h] - m_new[h]) * c) for h in heads]
        p = [jnp.exp2((s[h] - m_new[h]) * c) for h in heads]
        l = [alpha[h] * l[h] + _colsum(p[h]) for h in heads]
        pv = [_dot(vt_ref[h, j], p[h].astype(BF16)) for h in heads]
        for h in heads:
            acc_ref[h] = alpha[h] * acc_ref[h] + pv[h]
        return tuple(m_new) + tuple(l)

    carry = lax.fori_loop(0, qi, body, tuple(m) + tuple(l))
    l = carry[hp:]
    for h in heads:
        o_ref[:, h * dv:(h + 1) * dv] = (acc_ref[h] / l[h]).T.astype(o_ref.dtype)


def _mla_attention(q, k, v, seq, hp=4):
    nq = seq // ATT_TQ
    in_specs = [pl.BlockSpec((ATT_TQ, hp * MLA_QK_PAD), lambda b, h, i: (b * nq + i, h)),
                pl.BlockSpec((seq, hp * MLA_QK_PAD), lambda b, h, i: (b, h)),
                pl.BlockSpec((seq, hp * MLA_V), lambda b, h, i: (b, h))]
    return _attn_call(functools.partial(_mla_attn_kernel, hp=hp), (q, k, v), in_specs,
                      MLA_V, MLA_V, 1, hp, seq, MLA_HEADS, "mla_attention")


def _sb_attn_kernel(q_ref, k_ref, v_ref, o_ref, vt_ref, acc_ref, *, hp):
    qi = pl.program_id(2)
    d = SB_DIM
    heads = range(hp)

    @pl.when(qi == 0)
    def _():
        _stage_vt(v_ref, vt_ref, hp, d)

    scale = 1.0 / math.sqrt(SB_DIM)
    qt = [q_ref[:, h * d:(h + 1) * d].T for h in heads]
    key, qry = _tile_iotas()
    strict = key < qry
    later = (qry > key).astype(BF16)
    later2 = jnp.concatenate([later, later], axis=1)

    def block(j, r, diagonal):
        z2 = [_dot(_key_block(k_ref, j, h * d, d), qt[h]) * (scale * LOG2E) for h in heads]
        nz2 = [-x for x in z2]
        t = [jnp.log2(1.0 + jnp.exp2(jnp.minimum(z2[h], nz2[h]))) for h in heads]
        log_beta = [jnp.minimum(z2[h], 0.0) - t[h] for h in heads]
        neg_keep = [jnp.maximum(z2[h], 0.0) + t[h] for h in heads]
        if diagonal:
            neg_keep = [jnp.where(strict, x, 0.0) for x in neg_keep]
        hi = [x.astype(BF16) for x in neg_keep]
        lo = [(neg_keep[h] - hi[h].astype(F32)).astype(BF16) for h in heads]
        after = [_dot(later2, jnp.concatenate([hi[h], lo[h]], axis=0)) for h in heads]
        a = [jnp.exp2(log_beta[h] - after[h] - r[h]) for h in heads]
        if diagonal:
            a = [jnp.where(strict, x, 0.0) for x in a]
        pv = [_dot(vt_ref[h, j], a[h].astype(BF16)) for h in heads]
        r = [r[h] + after[h][0:1, :] + neg_keep[h][0:1, :] for h in heads]
        return r, pv

    r, pv = block(qi, [jnp.zeros((1, ATT_TQ), F32)] * hp, True)
    for h in heads:
        acc_ref[h] = pv[h]

    def body(it, r):
        r, pv = block(qi - 1 - it, list(r), False)
        for h in heads:
            acc_ref[h] += pv[h]
        return tuple(r)

    lax.fori_loop(0, qi, body, tuple(r))
    for h in heads:
        o_ref[:, h * d:(h + 1) * d] = acc_ref[h].T.astype(o_ref.dtype)


def _sb_attention(qkv, seq, hp=4):
    nq = seq // ATT_TQ
    groups = SB_HEADS // hp
    in_specs = [pl.BlockSpec((ATT_TQ, hp * SB_DIM), lambda b, h, i: (b * nq + i, h)),
                pl.BlockSpec((seq, hp * SB_DIM), lambda b, h, i: (b, groups + h)),
                pl.BlockSpec((seq, hp * SB_DIM), lambda b, h, i: (b, 2 * groups + h))]
    return _attn_call(functools.partial(_sb_attn_kernel, hp=hp), (qkv, qkv, qkv), in_specs,
                      SB_DIM, SB_DIM, 1, hp, seq, SB_HEADS, "sb_attention")


def _diff_attn_kernel(q_ref, k_ref, v_ref, slope_ref, lq1_ref, lk1_ref, lq2_ref, lk2_ref,
                      g_ref, o_ref, vt_ref, acc_ref, *, lambda_init, hp):
    qi = pl.program_id(2)
    d = DIFF_DIM
    dv = 2 * d
    maps = range(2 * hp)

    @pl.when(qi == 0)
    def _():
        _stage_vt(v_ref, vt_ref, hp, dv)

    inv_scale = math.sqrt(DIFF_DIM)
    c2 = LOG2E / inv_scale
    qt = [q_ref[:, c * d:(c + 1) * d].T for c in maps]
    slope = [slope_ref[c // 2, :, 0:1] * inv_scale for c in maps]
    key, qry = _tile_iotas()
    rel = (qry - key).astype(F32)
    visible = _chunk_visible(qry, key)
    bias_off = [-slope[2 * h] * rel for h in range(hp)]

    def scores(j):
        return [_dot(_key_block(k_ref, j, c * d, d), qt[c]) for c in maps]

    s = scores(qi)
    s = [jnp.where(visible, s[c] - slope[c] * jnp.abs(rel), -1e30) for c in maps]
    m = [_colmax(x) for x in s]
    p = [jnp.exp2((s[c] - m[c]) * c2) for c in maps]
    l = [_colsum(x) for x in p]
    for c in maps:
        acc_ref[c] = _dot(vt_ref[c // 2, qi], p[c].astype(BF16))

    def body(it, carry):
        m, l = carry[:2 * hp], carry[2 * hp:]
        j = qi - 1 - it
        dist = ((qi - j) * ATT_TK).astype(F32)
        s = scores(j)
        u = [s[c] + bias_off[c // 2] for c in maps]
        off = [-slope[c] * dist for c in maps]
        m_new = [jnp.maximum(m[c], _colmax(u[c]) + off[c]) for c in maps]
        alpha = [jnp.exp2((m[c] - m_new[c]) * c2) for c in maps]
        p = [jnp.exp2((u[c] - (m_new[c] - off[c])) * c2) for c in maps]
        l = [alpha[c] * l[c] + _colsum(p[c]) for c in maps]
        pv = [_dot(vt_ref[c // 2, j], p[c].astype(BF16)) for c in maps]
        for c in maps:
            acc_ref[c] = alpha[c] * acc_ref[c] + pv[c]
        return tuple(m_new) + tuple(l)

    carry = lax.fori_loop(0, qi, body, tuple(m) + tuple(l))
    l = carry[2 * hp:]

    lam = (jnp.exp(jnp.sum(lq1_ref[...] * lk1_ref[...], axis=-1, keepdims=True))
           - jnp.exp(jnp.sum(lq2_ref[...] * lk2_ref[...], axis=-1, keepdims=True))
           + lambda_init)
    for h in range(hp):
        ot = acc_ref[2 * h] / l[2 * h] - lam * (acc_ref[2 * h + 1] / l[2 * h + 1])
        yt = ot * lax.rsqrt(jnp.mean(ot * ot, axis=0, keepdims=True) + EPS)
        o_ref[:, h * dv:(h + 1) * dv] = (yt.T * g_ref[...] * (1.0 - lambda_init)).astype(o_ref.dtype)


def _diff_attention(qkv, slopes, lq1, lk1, lq2, lk2, subln, lambda_init, seq, hp=4):
    nq = seq // ATT_TQ
    dh = 2 * DIFF_DIM
    groups = DIFF_HEADS // hp
    vec = pl.BlockSpec((1, DIFF_DIM), lambda b, h, i: (0, 0))
    in_specs = [pl.BlockSpec((ATT_TQ, hp * dh), lambda b, h, i: (b * nq + i, h)),
                pl.BlockSpec((seq, hp * dh), lambda b, h, i: (b, groups + h)),
                pl.BlockSpec((seq, hp * dh), lambda b, h, i: (b, 2 * groups + h)),
                pl.BlockSpec((hp, 1, LANES), lambda b, h, i: (h, 0, 0)),
                vec, vec, vec, vec,
                pl.BlockSpec((1, dh), lambda b, h, i: (0, 0))]
    inputs = (qkv, qkv, qkv, slopes,
              lq1.reshape(1, -1).astype(F32), lk1.reshape(1, -1).astype(F32),
              lq2.reshape(1, -1).astype(F32), lk2.reshape(1, -1).astype(F32),
              subln.reshape(1, dh).astype(F32))
    return _attn_call(functools.partial(_diff_attn_kernel, lambda_init=lambda_init, hp=hp),
                      inputs, in_specs, dh, dh, 2, hp, seq, DIFF_HEADS, "diff_attention")


def _out_proj_kernel(a1_ref, a2_ref, b_ref, r_ref, o_ref):
    half = a1_ref.shape[1]
    acc = jnp.dot(a1_ref[...], b_ref[:half, :], preferred_element_type=F32)
    acc = acc + jnp.dot(a2_ref[...], b_ref[half:, :], preferred_element_type=F32)
    o_ref[...] = r_ref[...] + acc


def _out_proj(a1, a2, blk1, blk2, b, res, tm=1024, tn=512):
    m = res.shape[0]
    k, n = b.shape
    half = k // 2
    return pl.pallas_call(
        _out_proj_kernel,
        grid=(m // tm, n // tn),
        in_specs=[pl.BlockSpec((tm, half), lambda i, j: (i, blk1)),
                  pl.BlockSpec((tm, half), lambda i, j: (i, blk2)),
                  pl.BlockSpec((k, tn), lambda i, j: (0, j)),
                  pl.BlockSpec((tm, tn), lambda i, j: (i, j))],
        out_specs=pl.BlockSpec((tm, tn), lambda i, j: (i, j)),
        out_shape=jax.ShapeDtypeStruct((m, n), F32),
        compiler_params=_params(("arbitrary", "arbitrary"), 48),
        name="out_proj",
    )(a1, a2, b, res)


def _ffn_up_kernel(a_ref, bg_ref, bu_ref, cwg_ref, cwu_ref, cbg_ref, cbu_ref, o_ref,
                   h_ref, carry_ref, *, tm, tiles_per_seq):
    i = pl.program_id(0)
    j = pl.program_id(1)
    halo = FFN_HALO
    width = 2 * FFN_TN

    @pl.when(i % tiles_per_seq == 0)
    def _():
        h_ref[0:halo, :] = jnp.zeros((halo, width), F32)

    @pl.when(i % tiles_per_seq != 0)
    def _():
        h_ref[0:halo, :] = carry_ref[j]

    a = a_ref[...]
    h_ref[halo:halo + tm, 0:FFN_TN] = jnp.dot(a, bg_ref[...], preferred_element_type=F32)
    h_ref[halo:halo + tm, FFN_TN:width] = jnp.dot(a, bu_ref[...], preferred_element_type=F32)
    carry_ref[j] = h_ref[tm:tm + halo, :]

    wg, wu = cwg_ref[...], cwu_ref[...]
    w = [jnp.concatenate([wg[t:t + 1], wu[t:t + 1]], axis=1) for t in range(CONV_W)]
    bias = jnp.concatenate([cbg_ref[...], cbu_ref[...]], axis=1)
    rows = 128
    for c in range(tm // rows):
        x = h_ref[c * rows:c * rows + halo + rows, :]
        conv = bias
        for t in range(CONV_W):
            back = CONV_W - 1 - t
            shifted = pltpu.roll(x, back, 0) if back else x
            conv = conv + shifted[halo:, :] * w[t]
        g = conv[:, :FFN_TN]
        u = conv[:, FFN_TN:]
        o_ref[c * rows:(c + 1) * rows, :] = (g * jax.nn.sigmoid(g) * u).astype(o_ref.dtype)


def _ffn_up(a, w_up, conv_w, conv_b, layer, seq, tm=1024):
    m, k = a.shape
    nj = FFN_DIM // FFN_TN
    width = 2 * FFN_TN
    return pl.pallas_call(
        functools.partial(_ffn_up_kernel, tm=tm, tiles_per_seq=seq // tm),
        grid=(m // tm, nj),
        in_specs=[pl.BlockSpec((tm, k), lambda i, j: (i, 0)),
                  pl.BlockSpec((None, k, FFN_TN), lambda i, j: (layer, 0, j)),
                  pl.BlockSpec((None, k, FFN_TN), lambda i, j: (layer, 0, nj + j)),
                  pl.BlockSpec((None, CONV_W, FFN_TN), lambda i, j: (layer, 0, j)),
                  pl.BlockSpec((None, CONV_W, FFN_TN), lambda i, j: (layer, 0, nj + j)),
                  pl.BlockSpec((None, 1, FFN_TN), lambda i, j: (layer, 0, j)),
                  pl.BlockSpec((None, 1, FFN_TN), lambda i, j: (layer, 0, nj + j))],
        out_specs=pl.BlockSpec((tm, FFN_TN), lambda i, j: (i, j)),
        out_shape=jax.ShapeDtypeStruct((m, FFN_DIM), BF16),
        scratch_shapes=[pltpu.VMEM((tm + FFN_HALO, width), F32),
                        pltpu.VMEM((nj, FFN_HALO, width), F32)],
        compiler_params=_params(("arbitrary", "arbitrary"), 48),
        name="ffn_up_conv_gate",
    )(a, w_up, w_up, conv_w, conv_w, conv_b, conv_b)


def _ffn_down_kernel(a_ref, b_ref, r_ref, o_ref):
    o_ref[...] = r_ref[...] + jnp.dot(a_ref[...], b_ref[...], preferred_element_type=F32)


def _ffn_down(a, b, res, layer, tm=512, tn=512):
    m, k = a.shape
    n = b.shape[2]
    return pl.pallas_call(
        _ffn_down_kernel,
        grid=(m // tm, n // tn),
        in_specs=[pl.BlockSpec((tm, k), lambda i, j: (i, 0)),
                  pl.BlockSpec((None, k, tn), lambda i, j: (layer, 0, j)),
                  pl.BlockSpec((tm, tn), lambda i, j: (i, j))],
        out_specs=pl.BlockSpec((tm, tn), lambda i, j: (i, j)),
        out_shape=jax.ShapeDtypeStruct((m, n), F32),
        compiler_params=_params(("arbitrary", "arbitrary"), 56),
        name="ffn_down",
    )(a, b, res)


def _rope_tables(seq):
    half = MLA_ROPE // 2
    inv = ROPE_THETA ** (-jnp.arange(half, dtype=F32) / half)
    ang = jnp.arange(seq).astype(F32)[:, None] * inv[None, :]
    cos, sin = jnp.cos(ang), jnp.sin(ang)
    zero = jnp.zeros_like(cos)
    pad = jnp.zeros((seq, LANES - MLA_ROPE), F32)
    c = jnp.concatenate([cos, cos, pad], axis=1)
    s1 = jnp.concatenate([-sin, zero, pad], axis=1)
    s2 = jnp.concatenate([zero, sin, pad], axis=1)
    return c, s1, s2


def _conv_ffn(h, hn, w_up, conv_w, conv_b, w_down, layer, seq):
    act = _ffn_up(hn, w_up, conv_w, conv_b, layer, seq)
    return _ffn_down(act, w_down, h, layer)


def _mla_sb_layer(h, hn, w_in, q_norm, w_uq, kv_norm, w_ukv, w_out, tabs, seq):
    o1 = MLA_Q_RANK + MLA_KV_RANK
    o2 = o1 + MLA_ROPE
    w_lat = w_in[:, :o1].astype(BF16)
    w_kr = jnp.pad(w_in[:, o1:o2], ((0, 0), (0, LANES - MLA_ROPE))).astype(BF16)
    w_sb = w_in[:, o2:].astype(BF16)
    w_q = jnp.pad(w_uq.reshape(MLA_Q_RANK, MLA_HEADS, MLA_NOPE + MLA_ROPE),
                  ((0, 0), (0, 0), (0, MLA_QK_PAD - MLA_NOPE - MLA_ROPE)))
    w_q = w_q.reshape(MLA_Q_RANK, MLA_HEADS * MLA_QK_PAD).astype(BF16)
    w_kv = w_ukv.reshape(MLA_KV_RANK, MLA_HEADS, 2, MLA_NOPE).transpose(0, 2, 1, 3)
    w_kv = w_kv.reshape(MLA_KV_RANK, 2 * MLA_HEADS * MLA_NOPE).astype(BF16)

    lat = _matmul(hn, w_lat, F32, 1024, 768, "mla_latent_proj")
    krope = _krope(hn, w_kr, tabs)
    sb = _matmul(hn, w_sb, BF16, 1024, 1024, "sb_qkv_proj")
    q = _qup(lat, q_norm, w_q, tabs)
    k, v = _kvup(lat, kv_norm, w_kv, krope)
    o_a = _mla_attention(q, k, v, seq)
    o_b = _sb_attention(sb, seq)
    return _out_proj(o_a, o_b, 0, 0, w_out.astype(BF16), h)


def _diff_layer(h, hn, w_in, lq1, lk1, lq2, lk2, subln, w_out, layer_idx, seq):
    lambda_init = 0.8 - 0.6 * math.exp(-0.3 * layer_idx)
    slopes = jnp.asarray([2.0 ** (-8.0 * (i + 1) / DIFF_HEADS) for i in range(DIFF_HEADS)], F32)
    slopes = jnp.broadcast_to(slopes[:, None, None], (DIFF_HEADS, 1, LANES))
    qkv = _matmul(hn, w_in.astype(BF16), BF16, 1024, 1024, "diff_qkv_proj")
    o = _diff_attention(qkv, slopes, lq1, lk1, lq2, lk2, subln, lambda_init, seq)
    return _out_proj(o, o, 0, 1, w_out.astype(BF16), h)


def kernel(x, norm_mix, norm_ffn, norm_final, ab_w_in, ab_q_norm, ab_w_uq, ab_kv_norm,
           ab_w_ukv, ab_w_out, c_w_in, c_lambda_q1, c_lambda_k1, c_lambda_q2, c_lambda_k2,
           c_subln, c_w_out, ffn_w_up, ffn_conv_w, ffn_conv_b, ffn_w_down):
    batch, seq, d = x.shape
    depth = norm_mix.shape[0]
    tabs = _rope_tables(seq)
    w_up = ffn_w_up.astype(BF16)
    w_down = ffn_w_down.astype(BF16)
    conv_w = ffn_conv_w.astype(F32)
    conv_b = ffn_conv_b.astype(F32).reshape(depth, 1, -1)
    h = x.reshape(batch * seq, d)
    for i in range(depth):
        j = i // 2
        hn = _rmsnorm(h, norm_mix[i], BF16)
        if i % 2 == 0:
            h = _mla_sb_layer(h, hn, ab_w_in[j], ab_q_norm[j], ab_w_uq[j], ab_kv_norm[j],
                              ab_w_ukv[j], ab_w_out[j], tabs, seq)
        else:
            h = _diff_layer(h, hn, c_w_in[j], c_lambda_q1[j], c_lambda_k1[j], c_lambda_q2[j],
                            c_lambda_k2[j], c_subln[j], c_w_out[j], i, seq)
        hn = _rmsnorm(h, norm_ffn[i], BF16)
        h = _conv_ffn(h, hn, w_up, conv_w, conv_b, w_down, i, seq)
    return _rmsnorm(h, norm_final, x.dtype).reshape(batch, seq, d)
```

```python
import functools
import math

import jax
import jax.numpy as jnp
from jax import lax
from jax.experimental import pallas as pl
from jax.experimental.pallas import tpu as pltpu

F32 = jnp.float32
BF16 = jnp.bfloat16

CHUNK = 64
EPS = 1e-6
ROPE_THETA = 10000.0
MLA_HEADS = 16
MLA_Q_RANK = 1024
MLA_KV_RANK = 512
MLA_NOPE = 128
MLA_ROPE = 64
MLA_V = 128
SB_HEADS = 16
SB_DIM = 128
DIFF_HEADS = 16
DIFF_DIM = 128
FFN_DIM = 11008
CONV_W = 3

LANES = 128
MLA_QK_PAD = 256
ATT_TQ = 512
ATT_TK = 256
ATT_SUB = ATT_TQ // ATT_TK
FFN_TN = 256
FFN_HALO = 8
LOG2E = math.log2(math.e)
MIB = 1024 * 1024


def _params(sem, vmem_mib):
    return pltpu.CompilerParams(dimension_semantics=sem, vmem_limit_bytes=vmem_mib * MIB)


def _rms(x, g):
    return x * lax.rsqrt(jnp.mean(x * x, axis=-1, keepdims=True) + EPS) * g


def _rope_slab(x, c, s1, s2):
    return x * c + pltpu.roll(x, 96, 1) * s1 + pltpu.roll(x, 32, 1) * s2


def _rmsnorm_kernel(x_ref, g_ref, o_ref):
    o_ref[...] = _rms(x_ref[...].astype(F32), g_ref[...]).astype(o_ref.dtype)


def _rmsnorm(x, g, out_dtype, tm=256):
    m, d = x.shape
    return pl.pallas_call(
        _rmsnorm_kernel,
        grid=(m // tm,),
        in_specs=[pl.BlockSpec((tm, d), lambda i: (i, 0)),
                  pl.BlockSpec((1, d), lambda i: (0, 0))],
        out_specs=pl.BlockSpec((tm, d), lambda i: (i, 0)),
        out_shape=jax.ShapeDtypeStruct((m, d), out_dtype),
        compiler_params=_params(("arbitrary",), 40),
        name="rmsnorm",
    )(x, g.reshape(1, d).astype(F32))


def _mm_kernel(a_ref, b_ref, o_ref):
    o_ref[...] = jnp.dot(a_ref[...], b_ref[...],
                         preferred_element_type=F32).astype(o_ref.dtype)


def _matmul(a, b, out_dtype, tm, tn, name):
    m, k = a.shape
    n = b.shape[1]
    return pl.pallas_call(
        _mm_kernel,
        grid=(m // tm, n // tn),
        in_specs=[pl.BlockSpec((tm, k), lambda i, j: (i, 0)),
                  pl.BlockSpec((k, tn), lambda i, j: (0, j))],
        out_specs=pl.BlockSpec((tm, tn), lambda i, j: (i, j)),
        out_shape=jax.ShapeDtypeStruct((m, n), out_dtype),
        compiler_params=_params(("arbitrary", "arbitrary"), 56),
        name=name,
    )(a, b)


def _krope_kernel(a_ref, b_ref, c_ref, s1_ref, s2_ref, o_ref):
    acc = jnp.dot(a_ref[...], b_ref[...], preferred_element_type=F32)
    o_ref[...] = _rope_slab(acc, c_ref[...], s1_ref[...], s2_ref[...]).astype(o_ref.dtype)


def _krope(a, b, tabs, tm=1024):
    m, k = a.shape
    nt = tabs[0].shape[0] // tm
    tab_spec = pl.BlockSpec((tm, LANES), lambda i: (i % nt, 0))
    return pl.pallas_call(
        _krope_kernel,
        grid=(m // tm,),
        in_specs=[pl.BlockSpec((tm, k), lambda i: (i, 0)),
                  pl.BlockSpec((k, LANES), lambda i: (0, 0)),
                  tab_spec, tab_spec, tab_spec],
        out_specs=pl.BlockSpec((tm, LANES), lambda i: (i, 0)),
        out_shape=jax.ShapeDtypeStruct((m, LANES), BF16),
        compiler_params=_params(("arbitrary",), 40),
        name="krope_proj",
    )(a, b, *tabs)


def _qup_kernel(a_ref, g_ref, b_ref, c_ref, s1_ref, s2_ref, o_ref, an_ref, *, heads_per_tile):
    @pl.when(pl.program_id(1) == 0)
    def _():
        an_ref[...] = _rms(a_ref[...], g_ref[...]).astype(BF16)

    acc = jnp.dot(an_ref[...], b_ref[...], preferred_element_type=F32)
    c, s1, s2 = c_ref[...], s1_ref[...], s2_ref[...]
    for h in range(heads_per_tile):
        lo = h * MLA_QK_PAD
        o_ref[:, lo:lo + LANES] = acc[:, lo:lo + LANES].astype(o_ref.dtype)
        o_ref[:, lo + LANES:lo + 2 * LANES] = _rope_slab(
            acc[:, lo + LANES:lo + 2 * LANES], c, s1, s2).astype(o_ref.dtype)


def _qup(lat, g, w, tabs, tm=1024, tn=1024):
    m = lat.shape[0]
    k = MLA_Q_RANK
    n = w.shape[1]
    nt = tabs[0].shape[0] // tm
    tab_spec = pl.BlockSpec((tm, LANES), lambda i, j: (i % nt, 0))
    return pl.pallas_call(
        functools.partial(_qup_kernel, heads_per_tile=tn // MLA_QK_PAD),
        grid=(m // tm, n // tn),
        in_specs=[pl.BlockSpec((tm, k), lambda i, j: (i, 0)),
                  pl.BlockSpec((1, k), lambda i, j: (0, 0)),
                  pl.BlockSpec((k, tn), lambda i, j: (0, j)),
                  tab_spec, tab_spec, tab_spec],
        out_specs=pl.BlockSpec((tm, tn), lambda i, j: (i, j)),
        out_shape=jax.ShapeDtypeStruct((m, n), BF16),
        scratch_shapes=[pltpu.VMEM((tm, k), BF16)],
        compiler_params=_params(("arbitrary", "arbitrary"), 48),
        name="mla_q_up",
    )(lat, g.reshape(1, k).astype(F32), w, *tabs)


def _kvup_kernel(a_ref, g_ref, b_ref, kr_ref, k_ref, v_ref):
    xn = _rms(a_ref[...], g_ref[...]).astype(BF16)
    acc = jnp.dot(xn, b_ref[...], preferred_element_type=F32)
    kr = kr_ref[...]
    for h in range(MLA_HEADS):
        lo = h * MLA_QK_PAD
        k_ref[:, lo:lo + LANES] = acc[:, h * LANES:(h + 1) * LANES].astype(k_ref.dtype)
        k_ref[:, lo + LANES:lo + 2 * LANES] = kr
    v_ref[...] = acc[:, MLA_HEADS * MLA_NOPE:].astype(v_ref.dtype)


def _kvup(lat, g, w, krope, tm=512):
    m = lat.shape[0]
    k = MLA_KV_RANK
    n = w.shape[1]
    lat_blk = MLA_Q_RANK // MLA_KV_RANK
    return pl.pallas_call(
        _kvup_kernel,
        grid=(m // tm,),
        in_specs=[pl.BlockSpec((tm, k), lambda i: (i, lat_blk)),
                  pl.BlockSpec((1, k), lambda i: (0, 0)),
                  pl.BlockSpec((k, n), lambda i: (0, 0)),
                  pl.BlockSpec((tm, LANES), lambda i: (i, 0))],
        out_specs=[pl.BlockSpec((tm, MLA_HEADS * MLA_QK_PAD), lambda i: (i, 0)),
                   pl.BlockSpec((tm, MLA_HEADS * MLA_V), lambda i: (i, 0))],
        out_shape=[jax.ShapeDtypeStruct((m, MLA_HEADS * MLA_QK_PAD), BF16),
                   jax.ShapeDtypeStruct((m, MLA_HEADS * MLA_V), BF16)],
        compiler_params=_params(("arbitrary",), 48),
        name="mla_kv_up",
    )(lat, g.reshape(1, k).astype(F32), w, krope)


def _tile_iotas():
    key = lax.broadcasted_iota(jnp.int32, (ATT_TK, ATT_TQ), 0)
    qry = lax.broadcasted_iota(jnp.int32, (ATT_TK, ATT_TQ), 1)
    return key, qry


def _chunk_visible(qry, key):
    shift = CHUNK.bit_length() - 1
    return jnp.right_shift(key, shift) <= jnp.right_shift(qry, shift)


def _stage_vt(v_ref, vt_ref, hp, dv):
    for h in range(hp):
        for c in range(v_ref.shape[0] // ATT_TK):
            vt_ref[h, c] = v_ref[c * ATT_TK:(c + 1) * ATT_TK, h * dv:(h + 1) * dv].T


def _key_block(k_ref, j, lo, width):
    start = pl.multiple_of(j * ATT_TK, ATT_TK)
    return k_ref[pl.ds(start, ATT_TK), lo:lo + width]


def _dot(a, b):
    return jnp.dot(a, b, preferred_element_type=F32)


def _colmax(x):
    return jnp.max(x, axis=0, keepdims=True)


def _colsum(x):
    return jnp.sum(x, axis=0, keepdims=True)


def _attn_call(kernel_fn, inputs, in_specs, d_out, dv, n_acc, hp, seq, heads, name):
    m = inputs[0].shape[0]
    nq = seq // ATT_TQ
    return pl.pallas_call(
        kernel_fn,
        grid=(m // seq, heads // hp, nq),
        in_specs=in_specs,
        out_specs=pl.BlockSpec((ATT_TQ, hp * d_out), lambda b, h, i: (b * nq + i, h)),
        out_shape=jax.ShapeDtypeStruct((m, heads * d_out), BF16),
        scratch_shapes=[pltpu.VMEM((hp, seq // ATT_TK, dv, ATT_TK), BF16),
                        pltpu.VMEM((n_acc * hp, dv, ATT_TQ), F32)],
        compiler_params=_params(("arbitrary", "arbitrary", "arbitrary"), 56),
        name=name,
    )(*inputs)


def _mla_attn_kernel(q_ref, k_ref, v_ref, o_ref, vt_ref, acc_ref, *, hp):
    qi = pl.program_id(2)
    dq, dv = MLA_QK_PAD, MLA_V
    heads = range(hp)

    @pl.when(qi == 0)
    def _():
        _stage_vt(v_ref, vt_ref, hp, dv)

    c = LOG2E / math.sqrt(MLA_NOPE + MLA_ROPE)
    qt = [q_ref[:, h * dq:(h + 1) * dq].T for h in heads]
    key, qry = _tile_iotas()
    base = qi * ATT_SUB

    def scores(j, key_offset=None):
        s = [_dot(_key_block(k_ref, j, h * dq, dq), qt[h]) for h in heads]
        if key_offset is None:
            return s
        visible = _chunk_visible(qry, key + key_offset)
        return [jnp.where(visible, x, -1e30) for x in s]

    s = scores(base, 0)
    m = [_colmax(x) for x in s]
    p = [jnp.exp2((s[h] - m[h]) * c) for h in heads]
    l = [_colsum(x) for x in p]
    for h in heads:
        acc_ref[h] = _dot(vt_ref[h, base], p[h].astype(BF16))

    def update(s, j, m, l):
        m_new = [jnp.maximum(m[h], _colmax(s[h])) for h in heads]
        alpha = [jnp.exp2((m[h] - m_new[h]) * c) for h in heads]
        p = [jnp.exp2((s[h] - m_new[h]) * c) for h in heads]
        l = [alpha[h] * l[h] + _colsum(p[h]) for h in heads]
        pv = [_dot(vt_ref[h, j], p[h].astype(BF16)) for h in heads]
        for h in heads:
            acc_ref[h] = alpha[h] * acc_ref[h] + pv[h]
        return m_new, l

    for sub in range(1, ATT_SUB):
        m, l = update(scores(base + sub, sub * ATT_TK), base + sub, m, l)

    def body(j, carry):
        m, l = update(scores(j), j, carry[:hp], carry[hp:])
        return tuple(m) + tuple(l)

    carry = lax.fori_loop(0, base, body, tuple(m) + tuple(l))
    l = carry[hp:]
    for h in heads:
        o_ref[:, h * dv:(h + 1) * dv] = (acc_ref[h] / l[h]).T.astype(o_ref.dtype)


def _mla_attention(q, k, v, seq, hp=4):
    nq = seq // ATT_TQ
    in_specs = [pl.BlockSpec((ATT_TQ, hp * MLA_QK_PAD), lambda b, h, i: (b * nq + i, h)),
                pl.BlockSpec((seq, hp * MLA_QK_PAD), lambda b, h, i: (b, h)),
                pl.BlockSpec((seq, hp * MLA_V), lambda b, h, i: (b, h))]
    return _attn_call(functools.partial(_mla_attn_kernel, hp=hp), (q, k, v), in_specs,
                      MLA_V, MLA_V, 1, hp, seq, MLA_HEADS, "mla_attention")


def _sb_attn_kernel(q_ref, k_ref, v_ref, o_ref, vt_ref, acc_ref, *, hp):
    qi = pl.program_id(2)
    d = SB_DIM
    heads = range(hp)

    @pl.when(qi == 0)
    def _():
        _stage_vt(v_ref, vt_ref, hp, d)

    scale = 1.0 / math.sqrt(SB_DIM)
    qt = [q_ref[:, h * d:(h + 1) * d].T for h in heads]
    key, qry = _tile_iotas()
    base = qi * ATT_SUB
    later = (lax.broadcasted_iota(jnp.int32, (ATT_TK, ATT_TK), 1)
             > lax.broadcasted_iota(jnp.int32, (ATT_TK, ATT_TK), 0)).astype(BF16)
    later2 = jnp.concatenate([later, later], axis=1)

    def block(j, r, key_offset=None):
        diagonal = key_offset is not None
        if diagonal:
            strict = key + key_offset < qry
        z2 = [_dot(_key_block(k_ref, j, h * d, d), qt[h]) * (scale * LOG2E) for h in heads]
        nz2 = [-x for x in z2]
        t = [jnp.log2(1.0 + jnp.exp2(jnp.minimum(z2[h], nz2[h]))) for h in heads]
        log_beta = [jnp.minimum(z2[h], 0.0) - t[h] for h in heads]
        neg_keep = [jnp.maximum(z2[h], 0.0) + t[h] for h in heads]
        if diagonal:
            neg_keep = [jnp.where(strict, x, 0.0) for x in neg_keep]
        hi = [x.astype(BF16) for x in neg_keep]
        lo = [(neg_keep[h] - hi[h].astype(F32)).astype(BF16) for h in heads]
        after = [_dot(later2, jnp.concatenate([hi[h], lo[h]], axis=0)) for h in heads]
        a = [jnp.exp2(log_beta[h] - after[h] - r[h]) for h in heads]
        if diagonal:
            a = [jnp.where(strict, x, 0.0) for x in a]
        pv = [_dot(vt_ref[h, j], a[h].astype(BF16)) for h in heads]
        r = [r[h] + after[h][0:1, :] + neg_keep[h][0:1, :] for h in heads]
        return r, pv

    r = [jnp.zeros((1, ATT_TQ), F32)] * hp
    for sub in reversed(range(ATT_SUB)):
        r, pv = block(base + sub, r, sub * ATT_TK)
        for h in heads:
            if sub == ATT_SUB - 1:
                acc_ref[h] = pv[h]
            else:
                acc_ref[h] += pv[h]

    def body(it, r):
        r, pv = block(base - 1 - it, list(r))
        for h in heads:
            acc_ref[h] += pv[h]
        return tuple(r)

    lax.fori_loop(0, base, body, tuple(r))
    for h in heads:
        o_ref[:, h * d:(h + 1) * d] = acc_ref[h].T.astype(o_ref.dtype)


def _sb_attention(qkv, seq, hp=4):
    nq = seq // ATT_TQ
    groups = SB_HEADS // hp
    in_specs = [pl.BlockSpec((ATT_TQ, hp * SB_DIM), lambda b, h, i: (b * nq + i, h)),
                pl.BlockSpec((seq, hp * SB_DIM), lambda b, h, i: (b, groups + h)),
                pl.BlockSpec((seq, hp * SB_DIM), lambda b, h, i: (b, 2 * groups + h))]
    return _attn_call(functools.partial(_sb_attn_kernel, hp=hp), (qkv, qkv, qkv), in_specs,
                      SB_DIM, SB_DIM, 1, hp, seq, SB_HEADS, "sb_attention")


def _diff_attn_kernel(q_ref, k_ref, v_ref, slope_ref, lq1_ref, lk1_ref, lq2_ref, lk2_ref,
                      g_ref, o_ref, vt_ref, acc_ref, *, lambda_init, hp):
    qi = pl.program_id(2)
    d = DIFF_DIM
    dv = 2 * d
    maps = range(2 * hp)

    @pl.when(qi == 0)
    def _():
        _stage_vt(v_ref, vt_ref, hp, dv)

    inv_scale = math.sqrt(DIFF_DIM)
    c2 = LOG2E / inv_scale
    qt = [q_ref[:, c * d:(c + 1) * d].T for c in maps]
    slope = [slope_ref[c // 2, :, 0:1] * inv_scale for c in maps]
    key, qry = _tile_iotas()
    base = qi * ATT_SUB
    rel = (qry - key).astype(F32)
    bias_off = [-slope[2 * h] * rel for h in range(hp)]

    def scores(j):
        return [_dot(_key_block(k_ref, j, c * d, d), qt[c]) for c in maps]

    def diag_scores(sub):
        s = scores(base + sub)
        visible = _chunk_visible(qry, key + sub * ATT_TK)
        dist = jnp.abs(rel - float(sub * ATT_TK))
        return [jnp.where(visible, s[c] - slope[c] * dist, -1e30) for c in maps]

    s = diag_scores(0)
    m = [_colmax(x) for x in s]
    p = [jnp.exp2((s[c] - m[c]) * c2) for c in maps]
    l = [_colsum(x) for x in p]
    for c in maps:
        acc_ref[c] = _dot(vt_ref[c // 2, base], p[c].astype(BF16))

    def update(u, off, j, m, l):
        m_new = [jnp.maximum(m[c], _colmax(u[c]) + off[c]) for c in maps]
        alpha = [jnp.exp2((m[c] - m_new[c]) * c2) for c in maps]
        p = [jnp.exp2((u[c] - (m_new[c] - off[c])) * c2) for c in maps]
        l = [alpha[c] * l[c] + _colsum(p[c]) for c in maps]
        pv = [_dot(vt_ref[c // 2, j], p[c].astype(BF16)) for c in maps]
        for c in maps:
            acc_ref[c] = alpha[c] * acc_ref[c] + pv[c]
        return m_new, l

    zero = jnp.zeros((1, 1), F32)
    for sub in range(1, ATT_SUB):
        m, l = update(diag_scores(sub), [zero] * (2 * hp), base + sub, m, l)

    def body(it, carry):
        j = base - 1 - it
        dist = ((base - j) * ATT_TK).astype(F32)
        s = scores(j)
        u = [s[c] + bias_off[c // 2] for c in maps]
        m, l = update(u, [-slope[c] * dist for c in maps], j, carry[:2 * hp], carry[2 * hp:])
        return tuple(m) + tuple(l)

    carry = lax.fori_loop(0, base, body, tuple(m) + tuple(l))
    l = carry[2 * hp:]

    lam = (jnp.exp(jnp.sum(lq1_ref[...] * lk1_ref[...], axis=-1, keepdims=True))
           - jnp.exp(jnp.sum(lq2_ref[...] * lk2_ref[...], axis=-1, keepdims=True))
           + lambda_init)
    for h in range(hp):
        ot = acc_ref[2 * h] / l[2 * h] - lam * (acc_ref[2 * h + 1] / l[2 * h + 1])
        yt = ot * lax.rsqrt(jnp.mean(ot * ot, axis=0, keepdims=True) + EPS)
        o_ref[:, h * dv:(h + 1) * dv] = (yt.T * g_ref[...] * (1.0 - lambda_init)).astype(o_ref.dtype)


def _diff_attention(qkv, slopes, lq1, lk1, lq2, lk2, subln, lambda_init, seq, hp=2):
    nq = seq // ATT_TQ
    dh = 2 * DIFF_DIM
    groups = DIFF_HEADS // hp
    vec = pl.BlockSpec((1, DIFF_DIM), lambda b, h, i: (0, 0))
    in_specs = [pl.BlockSpec((ATT_TQ, hp * dh), lambda b, h, i: (b * nq + i, h)),
                pl.BlockSpec((seq, hp * dh), lambda b, h, i: (b, groups + h)),
                pl.BlockSpec((seq, hp * dh), lambda b, h, i: (b, 2 * groups + h)),
                pl.BlockSpec((hp, 1, LANES), lambda b, h, i: (h, 0, 0)),
                vec, vec, vec, vec,
                pl.BlockSpec((1, dh), lambda b, h, i: (0, 0))]
    inputs = (qkv, qkv, qkv, slopes,
              lq1.reshape(1, -1).astype(F32), lk1.reshape(1, -1).astype(F32),
              lq2.reshape(1, -1).astype(F32), lk2.reshape(1, -1).astype(F32),
              subln.reshape(1, dh).astype(F32))
    return _attn_call(functools.partial(_diff_attn_kernel, lambda_init=lambda_init, hp=hp),
                      inputs, in_specs, dh, dh, 2, hp, seq, DIFF_HEADS, "diff_attention")


def _out_proj_kernel(a1_ref, a2_ref, b_ref, r_ref, o_ref):
    half = a1_ref.shape[1]
    acc = jnp.dot(a1_ref[...], b_ref[:half, :], preferred_element_type=F32)
    acc = acc + jnp.dot(a2_ref[...], b_ref[half:, :], preferred_element_type=F32)
    o_ref[...] = r_ref[...] + acc


def _out_proj(a1, a2, blk1, blk2, b, res, tm=1024, tn=512):
    m = res.shape[0]
    k, n = b.shape
    half = k // 2
    return pl.pallas_call(
        _out_proj_kernel,
        grid=(m // tm, n // tn),
        in_specs=[pl.BlockSpec((tm, half), lambda i, j: (i, blk1)),
                  pl.BlockSpec((tm, half), lambda i, j: (i, blk2)),
                  pl.BlockSpec((k, tn), lambda i, j: (0, j)),
                  pl.BlockSpec((tm, tn), lambda i, j: (i, j))],
        out_specs=pl.BlockSpec((tm, tn), lambda i, j: (i, j)),
        out_shape=jax.ShapeDtypeStruct((m, n), F32),
        compiler_params=_params(("arbitrary", "arbitrary"), 48),
        name="out_proj",
    )(a1, a2, b, res)


def _ffn_up_kernel(a_ref, bg_ref, bu_ref, cwg_ref, cwu_ref, cbg_ref, cbu_ref, o_ref,
                   h_ref, carry_ref, *, tm, tiles_per_seq):
    i = pl.program_id(0)
    j = pl.program_id(1)
    halo = FFN_HALO
    width = 2 * FFN_TN

    @pl.when(i % tiles_per_seq == 0)
    def _():
        h_ref[0:halo, :] = jnp.zeros((halo, width), F32)

    @pl.when(i % tiles_per_seq != 0)
    def _():
        h_ref[0:halo, :] = carry_ref[j]

    a = a_ref[...]
    h_ref[halo:halo + tm, 0:FFN_TN] = jnp.dot(a, bg_ref[...], preferred_element_type=F32)
    h_ref[halo:halo + tm, FFN_TN:width] = jnp.dot(a, bu_ref[...], preferred_element_type=F32)
    carry_ref[j] = h_ref[tm:tm + halo, :]

    wg, wu = cwg_ref[...], cwu_ref[...]
    w = [jnp.concatenate([wg[t:t + 1], wu[t:t + 1]], axis=1) for t in range(CONV_W)]
    bias = jnp.concatenate([cbg_ref[...], cbu_ref[...]], axis=1)
    rows = 128
    for c in range(tm // rows):
        x = h_ref[c * rows:c * rows + halo + rows, :]
        conv = bias
        for t in range(CONV_W):
            back = CONV_W - 1 - t
            shifted = pltpu.roll(x, back, 0) if back else x
            conv = conv + shifted[halo:, :] * w[t]
        g = conv[:, :FFN_TN]
        u = conv[:, FFN_TN:]
        o_ref[c * rows:(c + 1) * rows, :] = (g * jax.nn.sigmoid(g) * u).astype(o_ref.dtype)


def _ffn_up(a, w_up, conv_w, conv_b, layer, seq, tm=1024):
    m, k = a.shape
    nj = FFN_DIM // FFN_TN
    width = 2 * FFN_TN
    return pl.pallas_call(
        functools.partial(_ffn_up_kernel, tm=tm, tiles_per_seq=seq // tm),
        grid=(m // tm, nj),
        in_specs=[pl.BlockSpec((tm, k), lambda i, j: (i, 0)),
                  pl.BlockSpec((None, k, FFN_TN), lambda i, j: (layer, 0, j)),
                  pl.BlockSpec((None, k, FFN_TN), lambda i, j: (layer, 0, nj + j)),
                  pl.BlockSpec((None, CONV_W, FFN_TN), lambda i, j: (layer, 0, j)),
                  pl.BlockSpec((None, CONV_W, FFN_TN), lambda i, j: (layer, 0, nj + j)),
                  pl.BlockSpec((None, 1, FFN_TN), lambda i, j: (layer, 0, j)),
                  pl.BlockSpec((None, 1, FFN_TN), lambda i, j: (layer, 0, nj + j))],
        out_specs=pl.BlockSpec((tm, FFN_TN), lambda i, j: (i, j)),
        out_shape=jax.ShapeDtypeStruct((m, FFN_DIM), BF16),
        scratch_shapes=[pltpu.VMEM((tm + FFN_HALO, width), F32),
                        pltpu.VMEM((nj, FFN_HALO, width), F32)],
        compiler_params=_params(("arbitrary", "arbitrary"), 48),
        name="ffn_up_conv_gate",
    )(a, w_up, w_up, conv_w, conv_w, conv_b, conv_b)


def _ffn_down_kernel(a_ref, b_ref, r_ref, o_ref):
    o_ref[...] = r_ref[...] + jnp.dot(a_ref[...], b_ref[...], preferred_element_type=F32)


def _ffn_down(a, b, res, layer, tm=512, tn=512):
    m, k = a.shape
    n = b.shape[2]
    return pl.pallas_call(
        _ffn_down_kernel,
        grid=(m // tm, n // tn),
        in_specs=[pl.BlockSpec((tm, k), lambda i, j: (i, 0)),
                  pl.BlockSpec((None, k, tn), lambda i, j: (layer, 0, j)),
                  pl.BlockSpec((tm, tn), lambda i, j: (i, j))],
        out_specs=pl.BlockSpec((tm, tn), lambda i, j: (i, j)),
        out_shape=jax.ShapeDtypeStruct((m, n), F32),
        compiler_params=_params(("arbitrary", "arbitrary"), 56),
        name="ffn_down",
    )(a, b, res)


def _rope_tables(seq):
    half = MLA_ROPE // 2
    inv = ROPE_THETA ** (-jnp.arange(half, dtype=F32) / half)
    ang = jnp.arange(seq).astype(F32)[:, None] * inv[None, :]
    cos, sin = jnp.cos(ang), jnp.sin(ang)
    zero = jnp.zeros_like(cos)
    pad = jnp.zeros((seq, LANES - MLA_ROPE), F32)
    c = jnp.concatenate([cos, cos, pad], axis=1)
    s1 = jnp.concatenate([-sin, zero, pad], axis=1)
    s2 = jnp.concatenate([zero, sin, pad], axis=1)
    return c, s1, s2


def _conv_ffn(h, hn, w_up, conv_w, conv_b, w_down, layer, seq):
    act = _ffn_up(hn, w_up, conv_w, conv_b, layer, seq)
    return _ffn_down(act, w_down, h, layer)


def _mla_sb_layer(h, hn, w_in, q_norm, w_uq, kv_norm, w_ukv, w_out, tabs, seq):
    o1 = MLA_Q_RANK + MLA_KV_RANK
    o2 = o1 + MLA_ROPE
    w_lat = w_in[:, :o1].astype(BF16)
    w_kr = jnp.pad(w_in[:, o1:o2], ((0, 0), (0, LANES - MLA_ROPE))).astype(BF16)
    w_sb = w_in[:, o2:].astype(BF16)
    w_q = jnp.pad(w_uq.reshape(MLA_Q_RANK, MLA_HEADS, MLA_NOPE + MLA_ROPE),
                  ((0, 0), (0, 0), (0, MLA_QK_PAD - MLA_NOPE - MLA_ROPE)))
    w_q = w_q.reshape(MLA_Q_RANK, MLA_HEADS * MLA_QK_PAD).astype(BF16)
    w_kv = w_ukv.reshape(MLA_KV_RANK, MLA_HEADS, 2, MLA_NOPE).transpose(0, 2, 1, 3)
    w_kv = w_kv.reshape(MLA_KV_RANK, 2 * MLA_HEADS * MLA_NOPE).astype(BF16)

    lat = _matmul(hn, w_lat, F32, 1024, 768, "mla_latent_proj")
    krope = _krope(hn, w_kr, tabs)
    sb = _matmul(hn, w_sb, BF16, 1024, 1024, "sb_qkv_proj")
    q = _qup(lat, q_norm, w_q, tabs)
    k, v = _kvup(lat, kv_norm, w_kv, krope)
    o_a = _mla_attention(q, k, v, seq)
    o_b = _sb_attention(sb, seq)
    return _out_proj(o_a, o_b, 0, 0, w_out.astype(BF16), h)


def _diff_layer(h, hn, w_in, lq1, lk1, lq2, lk2, subln, w_out, layer_idx, seq):
    lambda_init = 0.8 - 0.6 * math.exp(-0.3 * layer_idx)
    slopes = jnp.asarray([2.0 ** (-8.0 * (i + 1) / DIFF_HEADS) for i in range(DIFF_HEADS)], F32)
    slopes = jnp.broadcast_to(slopes[:, None, None], (DIFF_HEADS, 1, LANES))
    qkv = _matmul(hn, w_in.astype(BF16), BF16, 1024, 1024, "diff_qkv_proj")
    o = _diff_attention(qkv, slopes, lq1, lk1, lq2, lk2, subln, lambda_init, seq)
    return _out_proj(o, o, 0, 1, w_out.astype(BF16), h)


def kernel(x, norm_mix, norm_ffn, norm_final, ab_w_in, ab_q_norm, ab_w_uq, ab_kv_norm,
           ab_w_ukv, ab_w_out, c_w_in, c_lambda_q1, c_lambda_k1, c_lambda_q2, c_lambda_k2,
           c_subln, c_w_out, ffn_w_up, ffn_conv_w, ffn_conv_b, ffn_w_down):
    batch, seq, d = x.shape
    depth = norm_mix.shape[0]
    tabs = _rope_tables(seq)
    w_up = ffn_w_up.astype(BF16)
    w_down = ffn_w_down.astype(BF16)
    conv_w = ffn_conv_w.astype(F32)
    conv_b = ffn_conv_b.astype(F32).reshape(depth, 1, -1)
    h = x.reshape(batch * seq, d)
    for i in range(depth):
        j = i // 2
        hn = _rmsnorm(h, norm_mix[i], BF16)
        if i % 2 == 0:
            h = _mla_sb_layer(h, hn, ab_w_in[j], ab_q_norm[j], ab_w_uq[j], ab_kv_norm[j],
                              ab_w_ukv[j], ab_w_out[j], tabs, seq)
        else:
            h = _diff_layer(h, hn, c_w_in[j], c_lambda_q1[j], c_lambda_k1[j], c_lambda_q2[j],
                            c_lambda_k2[j], c_subln[j], c_w_out[j], i, seq)
        hn = _rmsnorm(h, norm_ffn[i], BF16)
        h = _conv_ffn(h, hn, w_up, conv_w, conv_b, w_down, i, seq)
    return _rmsnorm(h, norm_final, x.dtype).reshape(batch, seq, d)
```

```python
import functools
import math

import jax
import jax.numpy as jnp
from jax import lax
from jax.experimental import pallas as pl
from jax.experimental.pallas import tpu as pltpu

F32 = jnp.float32
BF16 = jnp.bfloat16

CHUNK = 64
EPS = 1e-6
ROPE_THETA = 10000.0
MLA_HEADS = 16
MLA_Q_RANK = 1024
MLA_KV_RANK = 512
MLA_NOPE = 128
MLA_ROPE = 64
MLA_V = 128
SB_HEADS = 16
SB_DIM = 128
DIFF_HEADS = 16
DIFF_DIM = 128
FFN_DIM = 11008
CONV_W = 3

LANES = 128
MLA_QK_PAD = 256
ATT_TK = 256
FFN_TN = 256
FFN_HALO = 8
LOG2E = math.log2(math.e)
MIB = 1024 * 1024


def _params(sem, vmem_mib):
    return pltpu.CompilerParams(dimension_semantics=sem, vmem_limit_bytes=vmem_mib * MIB)


def _rms(x, g):
    return x * lax.rsqrt(jnp.mean(x * x, axis=-1, keepdims=True) + EPS) * g


def _rope_slab(x, c, s1, s2):
    return x * c + pltpu.roll(x, 96, 1) * s1 + pltpu.roll(x, 32, 1) * s2


def _rmsnorm_kernel(x_ref, g_ref, o_ref):
    o_ref[...] = _rms(x_ref[...].astype(F32), g_ref[...]).astype(o_ref.dtype)


def _rmsnorm(x, g, out_dtype, tm=256):
    m, d = x.shape
    return pl.pallas_call(
        _rmsnorm_kernel,
        grid=(m // tm,),
        in_specs=[pl.BlockSpec((tm, d), lambda i: (i, 0)),
                  pl.BlockSpec((1, d), lambda i: (0, 0))],
        out_specs=pl.BlockSpec((tm, d), lambda i: (i, 0)),
        out_shape=jax.ShapeDtypeStruct((m, d), out_dtype),
        compiler_params=_params(("arbitrary",), 40),
        name="rmsnorm",
    )(x, g.reshape(1, d).astype(F32))


def _mm_kernel(a_ref, b_ref, o_ref):
    o_ref[...] = jnp.dot(a_ref[...], b_ref[...],
                         preferred_element_type=F32).astype(o_ref.dtype)


def _matmul(a, b, out_dtype, tm, tn, name):
    m, k = a.shape
    n = b.shape[1]
    return pl.pallas_call(
        _mm_kernel,
        grid=(m // tm, n // tn),
        in_specs=[pl.BlockSpec((tm, k), lambda i, j: (i, 0)),
                  pl.BlockSpec((k, tn), lambda i, j: (0, j))],
        out_specs=pl.BlockSpec((tm, tn), lambda i, j: (i, j)),
        out_shape=jax.ShapeDtypeStruct((m, n), out_dtype),
        compiler_params=_params(("arbitrary", "arbitrary"), 56),
        name=name,
    )(a, b)


def _krope_kernel(a_ref, b_ref, c_ref, s1_ref, s2_ref, o_ref):
    acc = jnp.dot(a_ref[...], b_ref[...], preferred_element_type=F32)
    o_ref[...] = _rope_slab(acc, c_ref[...], s1_ref[...], s2_ref[...]).astype(o_ref.dtype)


def _krope(a, b, tabs, tm=1024):
    m, k = a.shape
    nt = tabs[0].shape[0] // tm
    tab_spec = pl.BlockSpec((tm, LANES), lambda i: (i % nt, 0))
    return pl.pallas_call(
        _krope_kernel,
        grid=(m // tm,),
        in_specs=[pl.BlockSpec((tm, k), lambda i: (i, 0)),
                  pl.BlockSpec((k, LANES), lambda i: (0, 0)),
                  tab_spec, tab_spec, tab_spec],
        out_specs=pl.BlockSpec((tm, LANES), lambda i: (i, 0)),
        out_shape=jax.ShapeDtypeStruct((m, LANES), BF16),
        compiler_params=_params(("arbitrary",), 40),
        name="krope_proj",
    )(a, b, *tabs)


def _qup_kernel(a_ref, g_ref, b_ref, c_ref, s1_ref, s2_ref, o_ref, an_ref, *, heads_per_tile):
    @pl.when(pl.program_id(1) == 0)
    def _():
        an_ref[...] = _rms(a_ref[...], g_ref[...]).astype(BF16)

    acc = jnp.dot(an_ref[...], b_ref[...], preferred_element_type=F32)
    c, s1, s2 = c_ref[...], s1_ref[...], s2_ref[...]
    for h in range(heads_per_tile):
        lo = h * MLA_QK_PAD
        o_ref[:, lo:lo + LANES] = acc[:, lo:lo + LANES].astype(o_ref.dtype)
        o_ref[:, lo + LANES:lo + 2 * LANES] = _rope_slab(
            acc[:, lo + LANES:lo + 2 * LANES], c, s1, s2).astype(o_ref.dtype)


def _qup(lat, g, w, tabs, tm=1024, tn=1024):
    m = lat.shape[0]
    k = MLA_Q_RANK
    n = w.shape[1]
    nt = tabs[0].shape[0] // tm
    tab_spec = pl.BlockSpec((tm, LANES), lambda i, j: (i % nt, 0))
    return pl.pallas_call(
        functools.partial(_qup_kernel, heads_per_tile=tn // MLA_QK_PAD),
        grid=(m // tm, n // tn),
        in_specs=[pl.BlockSpec((tm, k), lambda i, j: (i, 0)),
                  pl.BlockSpec((1, k), lambda i, j: (0, 0)),
                  pl.BlockSpec((k, tn), lambda i, j: (0, j)),
                  tab_spec, tab_spec, tab_spec],
        out_specs=pl.BlockSpec((tm, tn), lambda i, j: (i, j)),
        out_shape=jax.ShapeDtypeStruct((m, n), BF16),
        scratch_shapes=[pltpu.VMEM((tm, k), BF16)],
        compiler_params=_params(("arbitrary", "arbitrary"), 48),
        name="mla_q_up",
    )(lat, g.reshape(1, k).astype(F32), w, *tabs)


def _kvup_kernel(a_ref, g_ref, b_ref, kr_ref, k_ref, v_ref):
    xn = _rms(a_ref[...], g_ref[...]).astype(BF16)
    acc = jnp.dot(xn, b_ref[...], preferred_element_type=F32)
    kr = kr_ref[...]
    for h in range(MLA_HEADS):
        lo = h * MLA_QK_PAD
        k_ref[:, lo:lo + LANES] = acc[:, h * LANES:(h + 1) * LANES].astype(k_ref.dtype)
        k_ref[:, lo + LANES:lo + 2 * LANES] = kr
    v_ref[...] = acc[:, MLA_HEADS * MLA_NOPE:].astype(v_ref.dtype)


def _kvup(lat, g, w, krope, tm=512):
    m = lat.shape[0]
    k = MLA_KV_RANK
    n = w.shape[1]
    lat_blk = MLA_Q_RANK // MLA_KV_RANK
    return pl.pallas_call(
        _kvup_kernel,
        grid=(m // tm,),
        in_specs=[pl.BlockSpec((tm, k), lambda i: (i, lat_blk)),
                  pl.BlockSpec((1, k), lambda i: (0, 0)),
                  pl.BlockSpec((k, n), lambda i: (0, 0)),
                  pl.BlockSpec((tm, LANES), lambda i: (i, 0))],
        out_specs=[pl.BlockSpec((tm, MLA_HEADS * MLA_QK_PAD), lambda i: (i, 0)),
                   pl.BlockSpec((tm, MLA_HEADS * MLA_V), lambda i: (i, 0))],
        out_shape=[jax.ShapeDtypeStruct((m, MLA_HEADS * MLA_QK_PAD), BF16),
                   jax.ShapeDtypeStruct((m, MLA_HEADS * MLA_V), BF16)],
        compiler_params=_params(("arbitrary",), 48),
        name="mla_kv_up",
    )(lat, g.reshape(1, k).astype(F32), w, krope)


def _tile_iotas(tq):
    key = lax.broadcasted_iota(jnp.int32, (ATT_TK, tq), 0)
    qry = lax.broadcasted_iota(jnp.int32, (ATT_TK, tq), 1)
    return key, qry


def _chunk_visible(qry, key):
    shift = CHUNK.bit_length() - 1
    return jnp.right_shift(key, shift) <= jnp.right_shift(qry, shift)


def _stage_vt(v_ref, vt_ref, hp, dv):
    for h in range(hp):
        for c in range(v_ref.shape[0] // ATT_TK):
            vt_ref[h, c] = v_ref[c * ATT_TK:(c + 1) * ATT_TK, h * dv:(h + 1) * dv].T


def _key_block(k_ref, j, lo, width):
    start = pl.multiple_of(j * ATT_TK, ATT_TK)
    return k_ref[pl.ds(start, ATT_TK), lo:lo + width]


def _dot(a, b):
    return jnp.dot(a, b, preferred_element_type=F32)


def _colmax(x):
    return jnp.max(x, axis=0, keepdims=True)


def _colsum(x):
    return jnp.sum(x, axis=0, keepdims=True)


def _attn_call(kernel_fn, inputs, in_specs, d_out, dv, n_acc, hp, tq, seq, heads, name):
    m = inputs[0].shape[0]
    nq = seq // tq
    return pl.pallas_call(
        kernel_fn,
        grid=(m // seq, heads // hp, nq),
        in_specs=in_specs,
        out_specs=pl.BlockSpec((tq, hp * d_out), lambda b, h, i: (b * nq + i, h)),
        out_shape=jax.ShapeDtypeStruct((m, heads * d_out), BF16),
        scratch_shapes=[pltpu.VMEM((hp, seq // ATT_TK, dv, ATT_TK), BF16),
                        pltpu.VMEM((n_acc * hp, dv, tq), F32)],
        compiler_params=_params(("arbitrary", "arbitrary", "arbitrary"), 56),
        name=name,
    )(*inputs)


def _mla_attn_kernel(q_ref, k_ref, v_ref, o_ref, vt_ref, acc_ref, *, hp):
    qi = pl.program_id(2)
    dq, dv = MLA_QK_PAD, MLA_V
    heads = range(hp)

    @pl.when(qi == 0)
    def _():
        _stage_vt(v_ref, vt_ref, hp, dv)

    c = LOG2E / math.sqrt(MLA_NOPE + MLA_ROPE)
    qt = [q_ref[:, h * dq:(h + 1) * dq].T for h in heads]
    tq = q_ref.shape[0]
    nsub = tq // ATT_TK
    key, qry = _tile_iotas(tq)
    base = qi * nsub

    def scores(j, key_offset=None):
        s = [_dot(_key_block(k_ref, j, h * dq, dq), qt[h]) for h in heads]
        if key_offset is None:
            return s
        visible = _chunk_visible(qry, key + key_offset)
        return [jnp.where(visible, x, -1e30) for x in s]

    s = scores(base, 0)
    m = [_colmax(x) for x in s]
    p = [jnp.exp2((s[h] - m[h]) * c) for h in heads]
    l = [_colsum(x) for x in p]
    for h in heads:
        acc_ref[h] = _dot(vt_ref[h, base], p[h].astype(BF16))

    def update(s, j, m, l):
        m_new = [jnp.maximum(m[h], _colmax(s[h])) for h in heads]
        alpha = [jnp.exp2((m[h] - m_new[h]) * c) for h in heads]
        p = [jnp.exp2((s[h] - m_new[h]) * c) for h in heads]
        l = [alpha[h] * l[h] + _colsum(p[h]) for h in heads]
        pv = [_dot(vt_ref[h, j], p[h].astype(BF16)) for h in heads]
        for h in heads:
            acc_ref[h] = alpha[h] * acc_ref[h] + pv[h]
        return m_new, l

    for sub in range(1, nsub):
        m, l = update(scores(base + sub, sub * ATT_TK), base + sub, m, l)

    def body(j, carry):
        m, l = update(scores(j), j, carry[:hp], carry[hp:])
        return tuple(m) + tuple(l)

    carry = lax.fori_loop(0, base, body, tuple(m) + tuple(l))
    l = carry[hp:]
    for h in heads:
        o_ref[:, h * dv:(h + 1) * dv] = (acc_ref[h] / l[h]).T.astype(o_ref.dtype)


def _mla_attention(q, k, v, seq, hp=4, tq=512):
    nq = seq // tq
    in_specs = [pl.BlockSpec((tq, hp * MLA_QK_PAD), lambda b, h, i: (b * nq + i, h)),
                pl.BlockSpec((seq, hp * MLA_QK_PAD), lambda b, h, i: (b, h)),
                pl.BlockSpec((seq, hp * MLA_V), lambda b, h, i: (b, h))]
    return _attn_call(functools.partial(_mla_attn_kernel, hp=hp), (q, k, v), in_specs,
                      MLA_V, MLA_V, 1, hp, tq, seq, MLA_HEADS, "mla_attention")


def _sb_attn_kernel(q_ref, k_ref, v_ref, o_ref, vt_ref, acc_ref, *, hp):
    qi = pl.program_id(2)
    d = SB_DIM
    heads = range(hp)

    @pl.when(qi == 0)
    def _():
        _stage_vt(v_ref, vt_ref, hp, d)

    scale = 1.0 / math.sqrt(SB_DIM)
    qt = [q_ref[:, h * d:(h + 1) * d].T for h in heads]
    tq = q_ref.shape[0]
    nsub = tq // ATT_TK
    key, qry = _tile_iotas(tq)
    base = qi * nsub
    later = (lax.broadcasted_iota(jnp.int32, (ATT_TK, ATT_TK), 1)
             > lax.broadcasted_iota(jnp.int32, (ATT_TK, ATT_TK), 0)).astype(BF16)
    later2 = jnp.concatenate([later, later], axis=1)

    def block(j, r, key_offset=None):
        diagonal = key_offset is not None
        if diagonal:
            strict = key + key_offset < qry
        z2 = [_dot(_key_block(k_ref, j, h * d, d), qt[h]) * (scale * LOG2E) for h in heads]
        nz2 = [-x for x in z2]
        t = [jnp.log2(1.0 + jnp.exp2(jnp.minimum(z2[h], nz2[h]))) for h in heads]
        log_beta = [jnp.minimum(z2[h], 0.0) - t[h] for h in heads]
        neg_keep = [jnp.maximum(z2[h], 0.0) + t[h] for h in heads]
        if diagonal:
            neg_keep = [jnp.where(strict, x, 0.0) for x in neg_keep]
        hi = [x.astype(BF16) for x in neg_keep]
        lo = [(neg_keep[h] - hi[h].astype(F32)).astype(BF16) for h in heads]
        after = [_dot(later2, jnp.concatenate([hi[h], lo[h]], axis=0)) for h in heads]
        a = [jnp.exp2(log_beta[h] - after[h] - r[h]) for h in heads]
        if diagonal:
            a = [jnp.where(strict, x, 0.0) for x in a]
        pv = [_dot(vt_ref[h, j], a[h].astype(BF16)) for h in heads]
        r = [r[h] + after[h][0:1, :] + neg_keep[h][0:1, :] for h in heads]
        return r, pv

    r = [jnp.zeros((1, tq), F32)] * hp
    for sub in reversed(range(nsub)):
        r, pv = block(base + sub, r, sub * ATT_TK)
        for h in heads:
            if sub == nsub - 1:
                acc_ref[h] = pv[h]
            else:
                acc_ref[h] += pv[h]

    def body(it, r):
        r, pv = block(base - 1 - it, list(r))
        for h in heads:
            acc_ref[h] += pv[h]
        return tuple(r)

    lax.fori_loop(0, base, body, tuple(r))
    for h in heads:
        o_ref[:, h * d:(h + 1) * d] = acc_ref[h].T.astype(o_ref.dtype)


def _sb_attention(qkv, seq, hp=4, tq=512):
    nq = seq // tq
    groups = SB_HEADS // hp
    in_specs = [pl.BlockSpec((tq, hp *SB_DIM), lambda b, h, i: (b * nq + i, h)),
                pl.BlockSpec((seq, hp * SB_DIM), lambda b, h, i: (b, groups + h)),
                pl.BlockSpec((seq, hp * SB_DIM), lambda b, h, i: (b, 2 * groups + h))]
    return _attn_call(functools.partial(_sb_attn_kernel, hp=hp), (qkv, qkv, qkv), in_specs,
                      SB_DIM, SB_DIM, 1, hp, tq, seq, SB_HEADS, "sb_attention")


def _diff_attn_kernel(q_ref, k_ref, v_ref, slope_ref, lq1_ref, lk1_ref, lq2_ref, lk2_ref,
                      g_ref, o_ref, vt_ref, acc_ref, *, lambda_init, hp):
    qi = pl.program_id(2)
    d = DIFF_DIM
    dv = 2 * d
    maps = range(2 * hp)

    @pl.when(qi == 0)
    def _():
        _stage_vt(v_ref, vt_ref, hp, dv)

    inv_scale = math.sqrt(DIFF_DIM)
    c2 = LOG2E / inv_scale
    qt = [q_ref[:, c * d:(c + 1) * d].T for c in maps]
    slope = [slope_ref[c // 2, :, 0:1] * inv_scale for c in maps]
    tq = q_ref.shape[0]
    nsub = tq // ATT_TK
    key, qry = _tile_iotas(tq)
    base = qi * nsub
    rel = (qry - key).astype(F32)
    bias_off = [-slope[2 * h] * rel for h in range(hp)]

    def scores(j):
        return [_dot(_key_block(k_ref, j, c * d, d), qt[c]) for c in maps]

    def diag_scores(sub):
        s = scores(base + sub)
        visible = _chunk_visible(qry, key + sub * ATT_TK)
        dist = jnp.abs(rel - float(sub * ATT_TK))
        return [jnp.where(visible, s[c] - slope[c] * dist, -1e30) for c in maps]

    s = diag_scores(0)
    m = [_colmax(x) for x in s]
    p = [jnp.exp2((s[c] - m[c]) * c2) for c in maps]
    l = [_colsum(x) for x in p]
    for c in maps:
        acc_ref[c] = _dot(vt_ref[c // 2, base], p[c].astype(BF16))

    def update(u, off, j, m, l):
        m_new = [jnp.maximum(m[c], _colmax(u[c]) + off[c]) for c in maps]
        alpha = [jnp.exp2((m[c] - m_new[c]) * c2) for c in maps]
        p = [jnp.exp2((u[c] - (m_new[c] - off[c])) * c2) for c in maps]
        l = [alpha[c] * l[c] + _colsum(p[c]) for c in maps]
        pv = [_dot(vt_ref[c // 2, j], p[c].astype(BF16)) for c in maps]
        for c in maps:
            acc_ref[c] = alpha[c] * acc_ref[c] + pv[c]
        return m_new, l

    zero = jnp.zeros((1, 1), F32)
    for sub in range(1, nsub):
        m, l = update(diag_scores(sub), [zero] * (2 * hp), base + sub, m, l)

    def body(it, carry):
        j = base - 1 - it
        dist = ((base - j) * ATT_TK).astype(F32)
        s = scores(j)
        u = [s[c] + bias_off[c // 2] for c in maps]
        m, l = update(u, [-slope[c] * dist for c in maps], j, carry[:2 * hp], carry[2 * hp:])
        return tuple(m) + tuple(l)

    carry = lax.fori_loop(0, base, body, tuple(m) + tuple(l))
    l = carry[2 * hp:]

    lam = (jnp.exp(jnp.sum(lq1_ref[...] * lk1_ref[...], axis=-1, keepdims=True))
           - jnp.exp(jnp.sum(lq2_ref[...] * lk2_ref[...], axis=-1, keepdims=True))
           + lambda_init)
    for h in range(hp):
        ot = acc_ref[2 * h] / l[2 * h] - lam * (acc_ref[2 * h + 1] / l[2 * h + 1])
        yt = ot * lax.rsqrt(jnp.mean(ot * ot, axis=0, keepdims=True) + EPS)
        o_ref[:, h * dv:(h + 1) * dv] = (yt.T * g_ref[...] * (1.0 - lambda_init)).astype(o_ref.dtype)


def _diff_attention(qkv, slopes, lq1, lk1, lq2, lk2, subln, lambda_init, seq, hp=4, tq=256):
    nq = seq // tq
    dh = 2 * DIFF_DIM
    groups = DIFF_HEADS // hp
    vec = pl.BlockSpec((1, DIFF_DIM), lambda b, h, i: (0, 0))
    in_specs = [pl.BlockSpec((tq, hp *dh), lambda b, h, i: (b * nq + i, h)),
                pl.BlockSpec((seq, hp * dh), lambda b, h, i: (b, groups + h)),
                pl.BlockSpec((seq, hp * dh), lambda b, h, i: (b, 2 * groups + h)),
                pl.BlockSpec((hp, 1, LANES), lambda b, h, i: (h, 0, 0)),
                vec, vec, vec, vec,
                pl.BlockSpec((1, dh), lambda b, h, i: (0, 0))]
    inputs = (qkv, qkv, qkv, slopes,
              lq1.reshape(1, -1).astype(F32), lk1.reshape(1, -1).astype(F32),
              lq2.reshape(1, -1).astype(F32), lk2.reshape(1, -1).astype(F32),
              subln.reshape(1, dh).astype(F32))
    return _attn_call(functools.partial(_diff_attn_kernel, lambda_init=lambda_init, hp=hp),
                      inputs, in_specs, dh, dh, 2, hp, tq, seq, DIFF_HEADS, "diff_attention")


def _out_proj_kernel(a1_ref, a2_ref, b_ref, r_ref, o_ref):
    half = a1_ref.shape[1]
    acc = jnp.dot(a1_ref[...], b_ref[:half, :], preferred_element_type=F32)
    acc = acc + jnp.dot(a2_ref[...], b_ref[half:, :], preferred_element_type=F32)
    o_ref[...] = r_ref[...] + acc


def _out_proj(a1, a2, blk1, blk2, b, res, tm=1024, tn=512):
    m = res.shape[0]
    k, n = b.shape
    half = k // 2
    return pl.pallas_call(
        _out_proj_kernel,
        grid=(m // tm, n // tn),
        in_specs=[pl.BlockSpec((tm, half), lambda i, j: (i, blk1)),
                  pl.BlockSpec((tm, half), lambda i, j: (i, blk2)),
                  pl.BlockSpec((k, tn), lambda i, j: (0, j)),
                  pl.BlockSpec((tm, tn), lambda i, j: (i, j))],
        out_specs=pl.BlockSpec((tm, tn), lambda i, j: (i, j)),
        out_shape=jax.ShapeDtypeStruct((m, n), F32),
        compiler_params=_params(("arbitrary", "arbitrary"), 48),
        name="out_proj",
    )(a1, a2, b, res)


def _ffn_up_kernel(a_ref, bg_ref, bu_ref, cwg_ref, cwu_ref, cbg_ref, cbu_ref, o_ref,
                   h_ref, carry_ref, *, tm, tiles_per_seq):
    i = pl.program_id(0)
    j = pl.program_id(1)
    halo = FFN_HALO
    width = 2 * FFN_TN

    @pl.when(i % tiles_per_seq == 0)
    def _():
        h_ref[0:halo, :] = jnp.zeros((halo, width), F32)

    @pl.when(i % tiles_per_seq != 0)
    def _():
        h_ref[0:halo, :] = carry_ref[j]

    a = a_ref[...]
    h_ref[halo:halo + tm, 0:FFN_TN] = _dot(a, bg_ref[...].astype(BF16))
    h_ref[halo:halo + tm, FFN_TN:width] = _dot(a, bu_ref[...].astype(BF16))
    carry_ref[j] = h_ref[tm:tm + halo, :]

    wg, wu = cwg_ref[...], cwu_ref[...]
    w = [jnp.concatenate([wg[t:t + 1], wu[t:t + 1]], axis=1) for t in range(CONV_W)]
    bias = jnp.concatenate([cbg_ref[...], cbu_ref[...]], axis=1)
    rows = 128
    for c in range(tm // rows):
        x = h_ref[c * rows:c * rows + halo + rows, :]
        conv = bias
        for t in range(CONV_W):
            back = CONV_W - 1 - t
            shifted = pltpu.roll(x, back, 0) if back else x
            conv = conv + shifted[halo:, :] * w[t]
        g = conv[:, :FFN_TN]
        u = conv[:, FFN_TN:]
        o_ref[c * rows:(c + 1) * rows, :] = (g * jax.nn.sigmoid(g) * u).astype(o_ref.dtype)


def _ffn_up(a, w_up, conv_w, conv_b, layer, seq, tm=1024):
    m, k = a.shape
    nj = FFN_DIM // FFN_TN
    width = 2 * FFN_TN
    return pl.pallas_call(
        functools.partial(_ffn_up_kernel, tm=tm, tiles_per_seq=seq // tm),
        grid=(m // tm, nj),
        in_specs=[pl.BlockSpec((tm, k), lambda i, j: (i, 0)),
                  pl.BlockSpec((None, k, FFN_TN), lambda i, j: (layer, 0, j)),
                  pl.BlockSpec((None, k, FFN_TN), lambda i, j: (layer, 0, nj + j)),
                  pl.BlockSpec((None, CONV_W, FFN_TN), lambda i, j: (layer, 0, j)),
                  pl.BlockSpec((None, CONV_W, FFN_TN), lambda i, j: (layer, 0, nj + j)),
                  pl.BlockSpec((None, 1, FFN_TN), lambda i, j: (layer, 0, j)),
                  pl.BlockSpec((None, 1, FFN_TN), lambda i, j: (layer, 0, nj + j))],
        out_specs=pl.BlockSpec((tm, FFN_TN), lambda i, j: (i, j)),
        out_shape=jax.ShapeDtypeStruct((m, FFN_DIM), BF16),
        scratch_shapes=[pltpu.VMEM((tm + FFN_HALO, width), F32),
                        pltpu.VMEM((nj, FFN_HALO, width), F32)],
        compiler_params=_params(("arbitrary", "arbitrary"), 48),
        name="ffn_up_conv_gate",
    )(a, w_up, w_up, conv_w, conv_w, conv_b, conv_b)


def _ffn_down_kernel(a_ref, b_ref, r_ref, o_ref):
    o_ref[...] = r_ref[...] + jnp.dot(a_ref[...], b_ref[...], preferred_element_type=F32)


def _ffn_down(a, b, res, layer, tm=512, tn=512):
    m, k = a.shape
    n = b.shape[2]
    return pl.pallas_call(
        _ffn_down_kernel,
        grid=(m // tm, n // tn),
        in_specs=[pl.BlockSpec((tm, k), lambda i, j: (i, 0)),
                  pl.BlockSpec((None, k, tn), lambda i, j: (layer, 0, j)),
                  pl.BlockSpec((tm, tn), lambda i, j: (i, j))],
        out_specs=pl.BlockSpec((tm, tn), lambda i, j: (i, j)),
        out_shape=jax.ShapeDtypeStruct((m, n), F32),
        compiler_params=_params(("arbitrary", "arbitrary"), 56),
        name="ffn_down",
    )(a, b, res)


def _rope_tables(seq):
    half = MLA_ROPE // 2
    inv = ROPE_THETA ** (-jnp.arange(half, dtype=F32) / half)
    ang = jnp.arange(seq).astype(F32)[:, None] * inv[None, :]
    cos, sin = jnp.cos(ang), jnp.sin(ang)
    zero = jnp.zeros_like(cos)
    pad = jnp.zeros((seq, LANES - MLA_ROPE), F32)
    c = jnp.concatenate([cos, cos, pad], axis=1)
    s1 = jnp.concatenate([-sin, zero, pad], axis=1)
    s2 = jnp.concatenate([zero, sin, pad], axis=1)
    return c, s1, s2


def _conv_ffn(h, hn, w_up, conv_w, conv_b, w_down, layer, seq):
    act = _ffn_up(hn, w_up, conv_w, conv_b, layer, seq)
    return _ffn_down(act, w_down, h, layer)


def _mla_sb_layer(h, hn, w_in, q_norm, w_uq, kv_norm, w_ukv, w_out, tabs, seq):
    o1 = MLA_Q_RANK + MLA_KV_RANK
    o2 = o1 + MLA_ROPE
    w_lat = w_in[:, :o1].astype(BF16)
    w_kr = jnp.pad(w_in[:, o1:o2], ((0, 0), (0, LANES - MLA_ROPE))).astype(BF16)
    w_sb = w_in[:, o2:].astype(BF16)
    w_q = jnp.pad(w_uq.reshape(MLA_Q_RANK, MLA_HEADS, MLA_NOPE + MLA_ROPE),
                  ((0, 0), (0, 0), (0, MLA_QK_PAD - MLA_NOPE - MLA_ROPE)))
    w_q = w_q.reshape(MLA_Q_RANK, MLA_HEADS * MLA_QK_PAD).astype(BF16)
    w_kv = w_ukv.reshape(MLA_KV_RANK, MLA_HEADS, 2, MLA_NOPE).transpose(0, 2, 1, 3)
    w_kv = w_kv.reshape(MLA_KV_RANK, 2 * MLA_HEADS * MLA_NOPE).astype(BF16)

    lat = _matmul(hn, w_lat, F32, 1024, 768, "mla_latent_proj")
    krope = _krope(hn, w_kr, tabs)
    sb = _matmul(hn, w_sb, BF16, 1024, 1024, "sb_qkv_proj")
    q = _qup(lat, q_norm, w_q, tabs)
    k, v = _kvup(lat, kv_norm, w_kv, krope)
    o_a = _mla_attention(q, k, v, seq)
    o_b = _sb_attention(sb, seq)
    return _out_proj(o_a, o_b, 0, 0, w_out.astype(BF16), h)


def _diff_layer(h, hn, w_in, lq1, lk1, lq2, lk2, subln, w_out, layer_idx, seq):
    lambda_init = 0.8 - 0.6 * math.exp(-0.3 * layer_idx)
    slopes = jnp.asarray([2.0 ** (-8.0 * (i + 1) / DIFF_HEADS) for i in range(DIFF_HEADS)], F32)
    slopes = jnp.broadcast_to(slopes[:, None, None], (DIFF_HEADS, 1, LANES))
    qkv = _matmul(hn, w_in.astype(BF16), BF16, 1024, 1024, "diff_qkv_proj")
    o = _diff_attention(qkv, slopes, lq1, lk1, lq2, lk2, subln, lambda_init, seq)
    return _out_proj(o, o, 0, 1, w_out.astype(BF16), h)


def kernel(x, norm_mix, norm_ffn, norm_final, ab_w_in, ab_q_norm, ab_w_uq, ab_kv_norm,
           ab_w_ukv, ab_w_out, c_w_in, c_lambda_q1, c_lambda_k1, c_lambda_q2, c_lambda_k2,
           c_subln, c_w_out, ffn_w_up, ffn_conv_w, ffn_conv_b, ffn_w_down):
    batch, seq, d = x.shape
    depth = norm_mix.shape[0]
    tabs = _rope_tables(seq)
    w_up = ffn_w_up
    w_down = ffn_w_down.astype(BF16)
    conv_w = ffn_conv_w.astype(F32)
    conv_b = ffn_conv_b.astype(F32).reshape(depth, 1, -1)
    h = x.reshape(batch * seq, d)
    for i in range(depth):
        j = i // 2
        hn = _rmsnorm(h, norm_mix[i], BF16)
        if i % 2 == 0:
            h = _mla_sb_layer(h, hn, ab_w_in[j], ab_q_norm[j], ab_w_uq[j], ab_kv_norm[j],
                              ab_w_ukv[j], ab_w_out[j], tabs, seq)
        else:
            h = _diff_layer(h, hn, c_w_in[j], c_lambda_q1[j], c_lambda_k1[j], c_lambda_q2[j],
                            c_lambda_k2[j], c_subln[j], c_w_out[j], i, seq)
        hn = _rmsnorm(h, norm_ffn[i], BF16)
        h = _conv_ffn(h, hn, w_up, conv_w, conv_b, w_down, i, seq)
    return _rmsnorm(h, norm_final, x.dtype).reshape(batch, seq, d)
```

```python
import functools
import math

import jax
import jax.numpy as jnp
from jax import lax
from jax.experimental import pallas as pl
from jax.experimental.pallas import tpu as pltpu

F32 = jnp.float32
BF16 = jnp.bfloat16

CHUNK = 64
EPS = 1e-6
ROPE_THETA = 10000.0
MLA_HEADS = 16
MLA_Q_RANK = 1024
MLA_KV_RANK = 512
MLA_NOPE = 128
MLA_ROPE = 64
MLA_V = 128
SB_HEADS = 16
SB_DIM = 128
DIFF_HEADS = 16
DIFF_DIM = 128
FFN_DIM = 11008
CONV_W = 3

LANES = 128
MLA_QK_PAD = 256
FFN_TN = 256
FFN_HALO = 8
LOG2E = math.log2(math.e)
MIB = 1024 * 1024


def _params(sem, vmem_mib):
    return pltpu.CompilerParams(dimension_semantics=sem, vmem_limit_bytes=vmem_mib * MIB)


def _rms(x, g):
    return x * lax.rsqrt(jnp.mean(x * x, axis=-1, keepdims=True) + EPS) * g


def _rope_slab(x, c, s1, s2):
    return x * c + pltpu.roll(x, 96, 1) * s1 + pltpu.roll(x, 32, 1) * s2


def _rmsnorm_kernel(x_ref, g_ref, o_ref):
    o_ref[...] = _rms(x_ref[...].astype(F32), g_ref[...]).astype(o_ref.dtype)


def _rmsnorm(x, g, out_dtype, tm=256):
    m, d = x.shape
    return pl.pallas_call(
        _rmsnorm_kernel,
        grid=(m // tm,),
        in_specs=[pl.BlockSpec((tm, d), lambda i: (i, 0)),
                  pl.BlockSpec((1, d), lambda i: (0, 0))],
        out_specs=pl.BlockSpec((tm, d), lambda i: (i, 0)),
        out_shape=jax.ShapeDtypeStruct((m, d), out_dtype),
        compiler_params=_params(("arbitrary",), 40),
        name="rmsnorm",
    )(x, g.reshape(1, d).astype(F32))


def _mm_kernel(a_ref, b_ref, o_ref):
    o_ref[...] = jnp.dot(a_ref[...], b_ref[...],
                         preferred_element_type=F32).astype(o_ref.dtype)


def _matmul(a, b, out_dtype, tm, tn, name):
    m, k = a.shape
    n = b.shape[1]
    return pl.pallas_call(
        _mm_kernel,
        grid=(m // tm, n // tn),
        in_specs=[pl.BlockSpec((tm, k), lambda i, j: (i, 0)),
                  pl.BlockSpec((k, tn), lambda i, j: (0, j))],
        out_specs=pl.BlockSpec((tm, tn), lambda i, j: (i, j)),
        out_shape=jax.ShapeDtypeStruct((m, n), out_dtype),
        compiler_params=_params(("arbitrary", "arbitrary"), 56),
        name=name,
    )(a, b)


def _krope_kernel(a_ref, b_ref, c_ref, s1_ref, s2_ref, o_ref):
    acc = jnp.dot(a_ref[...], b_ref[...], preferred_element_type=F32)
    o_ref[...] = _rope_slab(acc, c_ref[...], s1_ref[...], s2_ref[...]).astype(o_ref.dtype)


def _krope(a, b, tabs, tm=1024):
    m, k = a.shape
    nt = tabs[0].shape[0] // tm
    tab_spec = pl.BlockSpec((tm, LANES), lambda i: (i % nt, 0))
    return pl.pallas_call(
        _krope_kernel,
        grid=(m // tm,),
        in_specs=[pl.BlockSpec((tm, k), lambda i: (i, 0)),
                  pl.BlockSpec((k, LANES), lambda i: (0, 0)),
                  tab_spec, tab_spec, tab_spec],
        out_specs=pl.BlockSpec((tm, LANES), lambda i: (i, 0)),
        out_shape=jax.ShapeDtypeStruct((m, LANES), BF16),
        compiler_params=_params(("arbitrary",), 40),
        name="krope_proj",
    )(a, b, *tabs)


def _qup_kernel(a_ref, g_ref, b_ref, c_ref, s1_ref, s2_ref, o_ref, an_ref, *, heads_per_tile):
    @pl.when(pl.program_id(1) == 0)
    def _():
        an_ref[...] = _rms(a_ref[...], g_ref[...]).astype(BF16)

    acc = jnp.dot(an_ref[...], b_ref[...], preferred_element_type=F32)
    c, s1, s2 = c_ref[...], s1_ref[...], s2_ref[...]
    for h in range(heads_per_tile):
        lo = h * MLA_QK_PAD
        o_ref[:, lo:lo + LANES] = acc[:, lo:lo + LANES].astype(o_ref.dtype)
        o_ref[:, lo + LANES:lo + 2 * LANES] = _rope_slab(
            acc[:, lo + LANES:lo + 2 * LANES], c, s1, s2).astype(o_ref.dtype)


def _qup(lat, g, w, tabs, tm=1024, tn=1024):
    m = lat.shape[0]
    k = MLA_Q_RANK
    n = w.shape[1]
    nt = tabs[0].shape[0] // tm
    tab_spec = pl.BlockSpec((tm, LANES), lambda i, j: (i % nt, 0))
    return pl.pallas_call(
        functools.partial(_qup_kernel, heads_per_tile=tn // MLA_QK_PAD),
        grid=(m // tm, n // tn),
        in_specs=[pl.BlockSpec((tm, k), lambda i, j: (i, 0)),
                  pl.BlockSpec((1, k), lambda i, j: (0, 0)),
                  pl.BlockSpec((k, tn), lambda i, j: (0, j)),
                  tab_spec, tab_spec, tab_spec],
        out_specs=pl.BlockSpec((tm, tn), lambda i, j: (i, j)),
        out_shape=jax.ShapeDtypeStruct((m, n), BF16),
        scratch_shapes=[pltpu.VMEM((tm, k), BF16)],
        compiler_params=_params(("arbitrary", "arbitrary"), 48),
        name="mla_q_up",
    )(lat, g.reshape(1, k).astype(F32), w, *tabs)


def _kvup_kernel(a_ref, g_ref, b_ref, kr_ref, k_ref, v_ref):
    xn = _rms(a_ref[...], g_ref[...]).astype(BF16)
    acc = jnp.dot(xn, b_ref[...], preferred_element_type=F32)
    kr = kr_ref[...]
    for h in range(MLA_HEADS):
        lo = h * MLA_QK_PAD
        k_ref[:, lo:lo + LANES] = acc[:, h * LANES:(h + 1) * LANES].astype(k_ref.dtype)
        k_ref[:, lo + LANES:lo + 2 * LANES] = kr
    v_ref[...] = acc[:, MLA_HEADS * MLA_NOPE:].astype(v_ref.dtype)


def _kvup(lat, g, w, krope, tm=512):
    m = lat.shape[0]
    k = MLA_KV_RANK
    n = w.shape[1]
    lat_blk = MLA_Q_RANK // MLA_KV_RANK
    return pl.pallas_call(
        _kvup_kernel,
        grid=(m // tm,),
        in_specs=[pl.BlockSpec((tm, k), lambda i: (i, lat_blk)),
                  pl.BlockSpec((1, k), lambda i: (0, 0)),
                  pl.BlockSpec((k, n), lambda i: (0, 0)),
                  pl.BlockSpec((tm, LANES), lambda i: (i, 0))],
        out_specs=[pl.BlockSpec((tm, MLA_HEADS * MLA_QK_PAD), lambda i: (i, 0)),
                   pl.BlockSpec((tm, MLA_HEADS * MLA_V), lambda i: (i, 0))],
        out_shape=[jax.ShapeDtypeStruct((m, MLA_HEADS * MLA_QK_PAD), BF16),
                   jax.ShapeDtypeStruct((m, MLA_HEADS * MLA_V), BF16)],
        compiler_params=_params(("arbitrary",), 48),
        name="mla_kv_up",
    )(lat, g.reshape(1, k).astype(F32), w, krope)


def _tile_iotas(tk, tq):
    key = lax.broadcasted_iota(jnp.int32, (tk, tq), 0)
    qry = lax.broadcasted_iota(jnp.int32, (tk, tq), 1)
    return key, qry


def _chunk_visible(qry, key):
    shift = CHUNK.bit_length() - 1
    return jnp.right_shift(key, shift) <= jnp.right_shift(qry, shift)


def _stage_vt(v_ref, vt_ref, hp, dv):
    tk = vt_ref.shape[-1]
    for h in range(hp):
        for c in range(v_ref.shape[0] // tk):
            vt_ref[h, c] = v_ref[c * tk:(c + 1) * tk, h * dv:(h + 1) * dv].T


def _key_block(k_ref, j, tk, lo, width):
    start = pl.multiple_of(j * tk, tk)
    return k_ref[pl.ds(start, tk), lo:lo + width]


def _dot(a, b):
    return jnp.dot(a, b, preferred_element_type=F32)


def _colmax(x):
    return jnp.max(x, axis=0, keepdims=True)


def _colsum(x):
    return jnp.sum(x, axis=0, keepdims=True)


def _attn_call(kernel_fn, inputs, in_specs, d_out, dv, n_acc, hp, tq, tk, seq, heads, name):
    m = inputs[0].shape[0]
    nq = seq // tq
    return pl.pallas_call(
        kernel_fn,
        grid=(m // seq, heads // hp, nq),
        in_specs=in_specs,
        out_specs=pl.BlockSpec((tq, hp * d_out), lambda b, h, i: (b * nq + i, h)),
        out_shape=jax.ShapeDtypeStruct((m, heads * d_out), BF16),
        scratch_shapes=[pltpu.VMEM((hp, seq // tk, dv, tk), BF16),
                        pltpu.VMEM((n_acc * hp, dv, tq), F32)],
        compiler_params=_params(("arbitrary", "arbitrary", "arbitrary"), 56),
        name=name,
    )(*inputs)


def _mla_attn_kernel(q_ref, k_ref, v_ref, o_ref, vt_ref, acc_ref, *, hp):
    qi = pl.program_id(2)
    dq, dv = MLA_QK_PAD, MLA_V
    heads = range(hp)

    @pl.when(qi == 0)
    def _():
        _stage_vt(v_ref, vt_ref, hp, dv)

    c = LOG2E / math.sqrt(MLA_NOPE + MLA_ROPE)
    qt = [q_ref[:, h * dq:(h + 1) * dq].T for h in heads]
    tq, tk = q_ref.shape[0], vt_ref.shape[-1]
    nsub = tq // tk
    key, qry = _tile_iotas(tk, tq)
    base = qi * nsub

    def scores(j, key_offset=None):
        s = [_dot(_key_block(k_ref, j, tk, h *dq, dq), qt[h]) for h in heads]
        if key_offset is None:
            return s
        visible = _chunk_visible(qry, key + key_offset)
        return [jnp.where(visible, x, -1e30) for x in s]

    s = scores(base, 0)
    m = [_colmax(x) for x in s]
    p = [jnp.exp2((s[h] - m[h]) * c) for h in heads]
    l = [_colsum(x) for x in p]
    for h in heads:
        acc_ref[h] = _dot(vt_ref[h, base], p[h].astype(BF16))

    def update(s, j, m, l):
        m_new = [jnp.maximum(m[h], _colmax(s[h])) for h in heads]
        alpha = [jnp.exp2((m[h] - m_new[h]) * c) for h in heads]
        p = [jnp.exp2((s[h] - m_new[h]) * c) for h in heads]
        l = [alpha[h] * l[h] + _colsum(p[h]) for h in heads]
        pv = [_dot(vt_ref[h, j], p[h].astype(BF16)) for h in heads]
        for h in heads:
            acc_ref[h] = alpha[h] * acc_ref[h] + pv[h]
        return m_new, l

    for sub in range(1, nsub):
        m, l = update(scores(base + sub, sub * tk), base + sub, m, l)

    def body(j, carry):
        m, l = update(scores(j), j, carry[:hp], carry[hp:])
        return tuple(m) + tuple(l)

    carry = lax.fori_loop(0, base, body, tuple(m) + tuple(l))
    l = carry[hp:]
    for h in heads:
        o_ref[:, h * dv:(h + 1) * dv] = (acc_ref[h] / l[h]).T.astype(o_ref.dtype)


def _mla_attention(q, k, v, seq, hp=4, tq=512, tk=512):
    nq = seq // tq
    in_specs = [pl.BlockSpec((tq, hp * MLA_QK_PAD), lambda b, h, i: (b * nq + i, h)),
                pl.BlockSpec((seq, hp * MLA_QK_PAD), lambda b, h, i: (b, h)),
                pl.BlockSpec((seq, hp * MLA_V), lambda b, h, i: (b, h))]
    return _attn_call(functools.partial(_mla_attn_kernel, hp=hp), (q, k, v), in_specs,
                      MLA_V, MLA_V, 1, hp, tq, tk, seq, MLA_HEADS, "mla_attention")


def _sb_attn_kernel(q_ref, k_ref, v_ref, o_ref, vt_ref, acc_ref, *, hp):
    qi = pl.program_id(2)
    d = SB_DIM
    heads = range(hp)

    @pl.when(qi == 0)
    def _():
        _stage_vt(v_ref, vt_ref, hp, d)

    scale = 1.0 / math.sqrt(SB_DIM)
    qt = [q_ref[:, h * d:(h + 1) * d].T for h in heads]
    tq, tk = q_ref.shape[0], vt_ref.shape[-1]
    nsub = tq // tk
    key, qry = _tile_iotas(tk, tq)
    base = qi * nsub
    later = (lax.broadcasted_iota(jnp.int32, (tk, tk), 1)
             > lax.broadcasted_iota(jnp.int32, (tk, tk), 0)).astype(BF16)
    later2 = jnp.concatenate([later, later], axis=1)

    def block(j, r, key_offset=None):
        diagonal = key_offset is not None
        if diagonal:
            strict = key + key_offset < qry
        z2 = [_dot(_key_block(k_ref, j, tk, h *d, d), qt[h]) * (scale * LOG2E) for h in heads]
        nz2 = [-x for x in z2]
        t = [jnp.log2(1.0 + jnp.exp2(jnp.minimum(z2[h], nz2[h]))) for h in heads]
        log_beta = [jnp.minimum(z2[h], 0.0) - t[h] for h in heads]
        neg_keep = [jnp.maximum(z2[h], 0.0) + t[h] for h in heads]
        if diagonal:
            neg_keep = [jnp.where(strict, x, 0.0) for x in neg_keep]
        hi = [x.astype(BF16) for x in neg_keep]
        lo = [(neg_keep[h] - hi[h].astype(F32)).astype(BF16) for h in heads]
        after = [_dot(later2, jnp.concatenate([hi[h], lo[h]], axis=0)) for h in heads]
        a = [jnp.exp2(log_beta[h] - after[h] - r[h]) for h in heads]
        if diagonal:
            a = [jnp.where(strict, x, 0.0) for x in a]
        pv = [_dot(vt_ref[h, j], a[h].astype(BF16)) for h in heads]
        r = [r[h] + after[h][0:1, :] + neg_keep[h][0:1, :] for h in heads]
        return r, pv

    r = [jnp.zeros((1, tq), F32)] * hp
    for sub in reversed(range(nsub)):
        r, pv = block(base + sub, r, sub * tk)
        for h in heads:
            if sub == nsub - 1:
                acc_ref[h] = pv[h]
            else:
                acc_ref[h] += pv[h]

    def body(it, r):
        r, pv = block(base - 1 - it, list(r))
        for h in heads:
            acc_ref[h] += pv[h]
        return tuple(r)

    lax.fori_loop(0, base, body, tuple(r))
    for h in heads:
        o_ref[:, h * d:(h + 1) * d] = acc_ref[h].T.astype(o_ref.dtype)


def _sb_attention(qkv, seq, hp=4, tq=512, tk=256):
    nq = seq // tq
    groups = SB_HEADS // hp
    in_specs = [pl.BlockSpec((tq, hp *SB_DIM), lambda b, h, i: (b * nq + i, h)),
                pl.BlockSpec((seq, hp * SB_DIM), lambda b, h, i: (b, groups + h)),
                pl.BlockSpec((seq, hp * SB_DIM), lambda b, h, i: (b, 2 * groups + h))]
    return _attn_call(functools.partial(_sb_attn_kernel, hp=hp), (qkv, qkv, qkv), in_specs,
                      SB_DIM, SB_DIM, 1, hp, tq, tk, seq, SB_HEADS, "sb_attention")


def _diff_attn_kernel(q_ref, k_ref, v_ref, slope_ref, lq1_ref, lk1_ref, lq2_ref, lk2_ref,
                      g_ref, o_ref, vt_ref, acc_ref, *, lambda_init, hp):
    qi = pl.program_id(2)
    d = DIFF_DIM
    dv = 2 * d
    maps = range(2 * hp)

    @pl.when(qi == 0)
    def _():
        _stage_vt(v_ref, vt_ref, hp, dv)

    inv_scale = math.sqrt(DIFF_DIM)
    c2 = LOG2E / inv_scale
    qt = [q_ref[:, c * d:(c + 1) * d].T for c in maps]
    slope = [slope_ref[c // 2, :, 0:1] * inv_scale for c in maps]
    tq, tk = q_ref.shape[0], vt_ref.shape[-1]
    nsub = tq // tk
    key, qry = _tile_iotas(tk, tq)
    base = qi * nsub
    rel = (qry - key).astype(F32)
    bias_off = [-slope[2 * h] * rel for h in range(hp)]

    def scores(j):
        return [_dot(_key_block(k_ref, j, tk, c * d, d), qt[c]) for c in maps]

    def diag_scores(sub):
        s = scores(base + sub)
        visible = _chunk_visible(qry, key + sub * tk)
        dist = jnp.abs(rel - float(sub * tk))
        return [jnp.where(visible, s[c] - slope[c] * dist, -1e30) for c in maps]

    s = diag_scores(0)
    m = [_colmax(x) for x in s]
    p = [jnp.exp2((s[c] - m[c]) * c2) for c in maps]
    l = [_colsum(x) for x in p]
    for c in maps:
        acc_ref[c] = _dot(vt_ref[c // 2, base], p[c].astype(BF16))

    def update(u, off, j, m, l):
        m_new = [jnp.maximum(m[c], _colmax(u[c]) + off[c]) for c in maps]
        alpha = [jnp.exp2((m[c] - m_new[c]) * c2) for c in maps]
        p = [jnp.exp2((u[c] - (m_new[c] - off[c])) * c2) for c in maps]
        l = [alpha[c] * l[c] + _colsum(p[c]) for c in maps]
        pv = [_dot(vt_ref[c // 2, j], p[c].astype(BF16)) for c in maps]
        for c in maps:
            acc_ref[c] = alpha[c] * acc_ref[c] + pv[c]
        return m_new, l

    zero = jnp.zeros((1, 1), F32)
    for sub in range(1, nsub):
        m, l = update(diag_scores(sub), [zero] * (2 * hp), base + sub, m, l)

    def body(it, carry):
        j = base - 1 - it
        dist = ((base - j) * tk).astype(F32)
        s = scores(j)
        u = [s[c] + bias_off[c // 2] for c in maps]
        m, l = update(u, [-slope[c] * dist for c in maps], j, carry[:2 * hp], carry[2 * hp:])
        return tuple(m) + tuple(l)

    carry = lax.fori_loop(0, base, body, tuple(m) + tuple(l))
    l = carry[2 * hp:]

    lam = (jnp.exp(jnp.sum(lq1_ref[...] * lk1_ref[...], axis=-1, keepdims=True))
           - jnp.exp(jnp.sum(lq2_ref[...] * lk2_ref[...], axis=-1, keepdims=True))
           + lambda_init)
    for h in range(hp):
        ot = acc_ref[2 * h] / l[2 * h] - lam * (acc_ref[2 * h + 1] / l[2 * h + 1])
        yt = ot * lax.rsqrt(jnp.mean(ot * ot, axis=0, keepdims=True) + EPS)
        o_ref[:, h * dv:(h + 1) * dv] = (yt.T * g_ref[...] * (1.0 - lambda_init)).astype(o_ref.dtype)


def _diff_attention(qkv, slopes, lq1, lk1, lq2, lk2, subln, lambda_init, seq, hp=2, tq=512, tk=512):
    nq = seq // tq
    dh = 2 * DIFF_DIM
    groups = DIFF_HEADS // hp
    vec = pl.BlockSpec((1, DIFF_DIM), lambda b, h, i: (0, 0))
    in_specs = [pl.BlockSpec((tq, hp *dh), lambda b, h, i: (b * nq + i, h)),
                pl.BlockSpec((seq, hp * dh), lambda b, h, i: (b, groups + h)),
                pl.BlockSpec((seq, hp * dh), lambda b, h, i: (b, 2 * groups + h)),
                pl.BlockSpec((hp, 1, LANES), lambda b, h, i: (h, 0, 0)),
                vec, vec, vec, vec,
                pl.BlockSpec((1, dh), lambda b, h, i: (0, 0))]
    inputs = (qkv, qkv, qkv, slopes,
              lq1.reshape(1, -1).astype(F32), lk1.reshape(1, -1).astype(F32),
              lq2.reshape(1, -1).astype(F32), lk2.reshape(1, -1).astype(F32),
              subln.reshape(1, dh).astype(F32))
    return _attn_call(functools.partial(_diff_attn_kernel, lambda_init=lambda_init, hp=hp),
                      inputs, in_specs, dh, dh, 2, hp, tq, tk, seq, DIFF_HEADS, "diff_attention")


def _out_proj_kernel(a1_ref, a2_ref, b_ref, r_ref, o_ref):
    half = a1_ref.shape[1]
    acc = jnp.dot(a1_ref[...], b_ref[:half, :], preferred_element_type=F32)
    acc = acc + jnp.dot(a2_ref[...], b_ref[half:, :], preferred_element_type=F32)
    o_ref[...] = r_ref[...] + acc


def _out_proj(a1, a2, blk1, blk2, b, res, tm=1024, tn=512):
    m = res.shape[0]
    k, n = b.shape
    half = k // 2
    return pl.pallas_call(
        _out_proj_kernel,
        grid=(m // tm, n // tn),
        in_specs=[pl.BlockSpec((tm, half), lambda i, j: (i, blk1)),
                  pl.BlockSpec((tm, half), lambda i, j: (i, blk2)),
                  pl.BlockSpec((k, tn), lambda i, j: (0, j)),
                  pl.BlockSpec((tm, tn), lambda i, j: (i, j))],
        out_specs=pl.BlockSpec((tm, tn), lambda i, j: (i, j)),
        out_shape=jax.ShapeDtypeStruct((m, n), F32),
        compiler_params=_params(("arbitrary", "arbitrary"), 48),
        name="out_proj",
    )(a1, a2, b, res)


def _ffn_up_kernel(a_ref, bg_ref, bu_ref, cwg_ref, cwu_ref, cbg_ref, cbu_ref, o_ref,
                   h_ref, carry_ref, *, tm, tiles_per_seq):
    i = pl.program_id(0)
    j = pl.program_id(1)
    halo = FFN_HALO
    width = 2 * FFN_TN

    @pl.when(i % tiles_per_seq == 0)
    def _():
        h_ref[0:halo, :] = jnp.zeros((halo, width), F32)

    @pl.when(i % tiles_per_seq != 0)
    def _():
        h_ref[0:halo, :] = carry_ref[j]

    a = a_ref[...]
    h_ref[halo:halo + tm, 0:FFN_TN] = _dot(a, bg_ref[...].astype(BF16))
    h_ref[halo:halo + tm, FFN_TN:width] = _dot(a, bu_ref[...].astype(BF16))
    carry_ref[j] = h_ref[tm:tm + halo, :]

    wg, wu = cwg_ref[...], cwu_ref[...]
    w = [jnp.concatenate([wg[t:t + 1], wu[t:t + 1]], axis=1) for t in range(CONV_W)]
    bias = jnp.concatenate([cbg_ref[...], cbu_ref[...]], axis=1)
    rows = 128
    for c in range(tm // rows):
        x = h_ref[c * rows:c * rows + halo + rows, :]
        conv = bias
        for t in range(CONV_W):
            back = CONV_W - 1 - t
            shifted = pltpu.roll(x, back, 0) if back else x
            conv = conv + shifted[halo:, :] * w[t]
        g = conv[:, :FFN_TN]
        u = conv[:, FFN_TN:]
        o_ref[c * rows:(c + 1) * rows, :] = (g * jax.nn.sigmoid(g) * u).astype(o_ref.dtype)


def _ffn_up(a, w_up, conv_w, conv_b, layer, seq, tm=1024):
    m, k = a.shape
    nj = FFN_DIM // FFN_TN
    width = 2 * FFN_TN
    return pl.pallas_call(
        functools.partial(_ffn_up_kernel, tm=tm, tiles_per_seq=seq // tm),
        grid=(m // tm, nj),
        in_specs=[pl.BlockSpec((tm, k), lambda i, j: (i, 0)),
                  pl.BlockSpec((None, k, FFN_TN), lambda i, j: (layer, 0, j)),
                  pl.BlockSpec((None, k, FFN_TN), lambda i, j: (layer, 0, nj + j)),
                  pl.BlockSpec((None, CONV_W, FFN_TN), lambda i, j: (layer, 0, j)),
                  pl.BlockSpec((None, CONV_W, FFN_TN), lambda i, j: (layer, 0, nj + j)),
                  pl.BlockSpec((None, 1, FFN_TN), lambda i, j: (layer, 0, j)),
                  pl.BlockSpec((None, 1, FFN_TN), lambda i, j: (layer, 0, nj + j))],
        out_specs=pl.BlockSpec((tm, FFN_TN), lambda i, j: (i, j)),
        out_shape=jax.ShapeDtypeStruct((m, FFN_DIM), BF16),
        scratch_shapes=[pltpu.VMEM((tm + FFN_HALO, width), F32),
                        pltpu.VMEM((nj, FFN_HALO, width), F32)],
        compiler_params=_params(("arbitrary", "arbitrary"), 48),
        name="ffn_up_conv_gate",
    )(a, w_up, w_up, conv_w, conv_w, conv_b, conv_b)


def _ffn_down_kernel(a_ref, b_ref, r_ref, o_ref):
    o_ref[...] = r_ref[...] + _dot(a_ref[...], b_ref[...].astype(BF16))


def _ffn_down(a, b, res, layer, tm=1024, tn=256):
    m, k = a.shape
    n = b.shape[2]
    return pl.pallas_call(
        _ffn_down_kernel,
        grid=(m // tm, n // tn),
        in_specs=[pl.BlockSpec((tm, k), lambda i, j: (i, 0), pipeline_mode=pl.Buffered(1)),
                  pl.BlockSpec((None, k, tn), lambda i, j: (layer, 0, j)),
                  pl.BlockSpec((tm, tn), lambda i, j: (i, j))],
        out_specs=pl.BlockSpec((tm, tn), lambda i, j: (i, j)),
        out_shape=jax.ShapeDtypeStruct((m, n), F32),
        compiler_params=_params(("arbitrary", "arbitrary"), 56),
        name="ffn_down",
    )(a, b, res)


def _rope_tables(seq):
    half = MLA_ROPE // 2
    inv = ROPE_THETA ** (-jnp.arange(half, dtype=F32) / half)
    ang = jnp.arange(seq).astype(F32)[:, None] * inv[None, :]
    cos, sin = jnp.cos(ang), jnp.sin(ang)
    zero = jnp.zeros_like(cos)
    pad = jnp.zeros((seq, LANES - MLA_ROPE), F32)
    c = jnp.concatenate([cos, cos, pad], axis=1)
    s1 = jnp.concatenate([-sin, zero, pad], axis=1)
    s2 = jnp.concatenate([zero, sin, pad], axis=1)
    return c, s1, s2


def _conv_ffn(h, hn, w_up, conv_w, conv_b, w_down, layer, seq):
    act = _ffn_up(hn, w_up, conv_w, conv_b, layer, seq)
    return _ffn_down(act, w_down, h, layer)


def _mla_sb_layer(h, hn, w_in, q_norm, w_uq, kv_norm, w_ukv, w_out, tabs, seq):
    o1 = MLA_Q_RANK + MLA_KV_RANK
    o2 = o1 + MLA_ROPE
    w_lat = w_in[:, :o1].astype(BF16)
    w_kr = jnp.pad(w_in[:, o1:o2], ((0, 0), (0, LANES - MLA_ROPE))).astype(BF16)
    w_sb = w_in[:, o2:].astype(BF16)
    w_q = jnp.pad(w_uq.reshape(MLA_Q_RANK, MLA_HEADS, MLA_NOPE + MLA_ROPE),
                  ((0, 0), (0, 0), (0, MLA_QK_PAD - MLA_NOPE - MLA_ROPE)))
    w_q = w_q.reshape(MLA_Q_RANK, MLA_HEADS * MLA_QK_PAD).astype(BF16)
    w_kv = w_ukv.reshape(MLA_KV_RANK, MLA_HEADS, 2, MLA_NOPE).transpose(0, 2, 1, 3)
    w_kv = w_kv.reshape(MLA_KV_RANK, 2 * MLA_HEADS * MLA_NOPE).astype(BF16)

    lat = _matmul(hn, w_lat, F32, 1024, 768, "mla_latent_proj")
    krope = _krope(hn, w_kr, tabs)
    sb = _matmul(hn, w_sb, BF16, 1024, 1024, "sb_qkv_proj")
    q = _qup(lat, q_norm, w_q, tabs)
    k, v = _kvup(lat, kv_norm, w_kv, krope)
    o_a = _mla_attention(q, k, v, seq)
    o_b = _sb_attention(sb, seq)
    return _out_proj(o_a, o_b, 0, 0, w_out.astype(BF16), h)


def _diff_layer(h, hn, w_in, lq1, lk1, lq2, lk2, subln, w_out, layer_idx, seq):
    lambda_init = 0.8 - 0.6 * math.exp(-0.3 * layer_idx)
    slopes = jnp.asarray([2.0 ** (-8.0 * (i + 1) / DIFF_HEADS) for i in range(DIFF_HEADS)], F32)
    slopes = jnp.broadcast_to(slopes[:, None, None], (DIFF_HEADS, 1, LANES))
    qkv = _matmul(hn, w_in.astype(BF16), BF16, 1024, 1024, "diff_qkv_proj")
    o = _diff_attention(qkv, slopes, lq1, lk1, lq2, lk2, subln, lambda_init, seq)
    return _out_proj(o, o, 0, 1, w_out.astype(BF16), h)


def kernel(x, norm_mix, norm_ffn, norm_final, ab_w_in, ab_q_norm, ab_w_uq, ab_kv_norm,
           ab_w_ukv, ab_w_out, c_w_in, c_lambda_q1, c_lambda_k1, c_lambda_q2, c_lambda_k2,
           c_subln, c_w_out, ffn_w_up, ffn_conv_w, ffn_conv_b, ffn_w_down):
    batch, seq, d = x.shape
    depth = norm_mix.shape[0]
    tabs = _rope_tables(seq)
    w_up = ffn_w_up
    w_down = ffn_w_down
    conv_w = ffn_conv_w.astype(F32)
    conv_b = ffn_conv_b.astype(F32).reshape(depth, 1, -1)
    h = x.reshape(batch * seq, d)
    for i in range(depth):
        j = i // 2
        hn = _rmsnorm(h, norm_mix[i], BF16)
        if i % 2 == 0:
            h = _mla_sb_layer(h, hn, ab_w_in[j], ab_q_norm[j], ab_w_uq[j], ab_kv_norm[j],
                              ab_w_ukv[j], ab_w_out[j], tabs, seq)
        else:
            h = _diff_layer(h, hn, c_w_in[j], c_lambda_q1[j], c_lambda_k1[j], c_lambda_q2[j],
                            c_lambda_k2[j], c_subln[j], c_w_out[j], i, seq)
        hn = _rmsnorm(h, norm_ffn[i], BF16)
        h = _conv_ffn(h, hn, w_up, conv_w, conv_b, w_down, i, seq)
    return _rmsnorm(h, norm_final, x.dtype).reshape(batch, seq, d)
```

```python
import functools
import math

import jax
import jax.numpy as jnp
from jax import lax
from jax.experimental import pallas as pl
from jax.experimental.pallas import tpu as pltpu

F32 = jnp.float32
BF16 = jnp.bfloat16

CHUNK = 64
EPS = 1e-6
ROPE_THETA = 10000.0
MLA_HEADS = 16
MLA_Q_RANK = 1024
MLA_KV_RANK = 512
MLA_NOPE = 128
MLA_ROPE = 64
MLA_V = 128
SB_HEADS = 16
SB_DIM = 128
DIFF_HEADS = 16
DIFF_DIM = 128
FFN_DIM = 11008
CONV_W = 3

LANES = 128
MLA_QK_PAD = 256
FFN_TN = 256
FFN_HALO = 8
LOG2E = math.log2(math.e)
MIB = 1024 * 1024


def _params(sem, vmem_mib):
    return pltpu.CompilerParams(dimension_semantics=sem, vmem_limit_bytes=vmem_mib * MIB)


def _rms(x, g):
    return x * lax.rsqrt(jnp.mean(x * x, axis=-1, keepdims=True) + EPS) * g


def _rope_slab(x, c, s1, s2):
    return x * c + pltpu.roll(x, 96, 1) * s1 + pltpu.roll(x, 32, 1) * s2


def _rmsnorm_kernel(x_ref, g_ref, o_ref):
    o_ref[...] = _rms(x_ref[...].astype(F32), g_ref[...]).astype(o_ref.dtype)


def _rmsnorm(x, g, out_dtype, tm=512):
    m, d = x.shape
    return pl.pallas_call(
        _rmsnorm_kernel,
        grid=(m // tm,),
        in_specs=[pl.BlockSpec((tm, d), lambda i: (i, 0)),
                  pl.BlockSpec((1, d), lambda i: (0, 0))],
        out_specs=pl.BlockSpec((tm, d), lambda i: (i, 0)),
        out_shape=jax.ShapeDtypeStruct((m, d), out_dtype),
        compiler_params=_params(("arbitrary",), 48),
        name="rmsnorm",
    )(x, g.reshape(1, d).astype(F32))


def _mm_kernel(a_ref, b_ref, o_ref):
    o_ref[...] = jnp.dot(a_ref[...], b_ref[...].astype(BF16),
                         preferred_element_type=F32).astype(o_ref.dtype)


def _matmul(a, b, out_dtype, tm, tn, name, a_buffers=2):
    m, k = a.shape
    n = b.shape[1]
    return pl.pallas_call(
        _mm_kernel,
        grid=(m // tm, n // tn),
        in_specs=[pl.BlockSpec((tm, k), lambda i, j: (i, 0),
                               pipeline_mode=pl.Buffered(a_buffers)),
                  pl.BlockSpec((k, tn), lambda i, j: (0, j))],
        out_specs=pl.BlockSpec((tm, tn), lambda i, j: (i, j)),
        out_shape=jax.ShapeDtypeStruct((m, n), out_dtype),
        compiler_params=_params(("arbitrary", "arbitrary"), 56),
        name=name,
    )(a, b)


def _krope_kernel(a_ref, b_ref, c_ref, s1_ref, s2_ref, o_ref):
    acc = jnp.dot(a_ref[...], b_ref[...], preferred_element_type=F32)
    o_ref[...] = _rope_slab(acc, c_ref[...], s1_ref[...], s2_ref[...]).astype(o_ref.dtype)


def _krope(a, b, tabs, tm=1024):
    m, k = a.shape
    nt = tabs[0].shape[0] // tm
    tab_spec = pl.BlockSpec((tm, LANES), lambda i: (i % nt, 0))
    return pl.pallas_call(
        _krope_kernel,
        grid=(m // tm,),
        in_specs=[pl.BlockSpec((tm, k), lambda i: (i, 0)),
                  pl.BlockSpec((k, LANES), lambda i: (0, 0)),
                  tab_spec, tab_spec, tab_spec],
        out_specs=pl.BlockSpec((tm, LANES), lambda i: (i, 0)),
        out_shape=jax.ShapeDtypeStruct((m, LANES), BF16),
        compiler_params=_params(("arbitrary",), 40),
        name="krope_proj",
    )(a, b, *tabs)


def _qup_kernel(a_ref, g_ref, b_ref, c_ref, s1_ref, s2_ref, o_ref, an_ref, *, heads_per_tile):
    @pl.when(pl.program_id(1) == 0)
    def _():
        an_ref[...] = _rms(a_ref[...], g_ref[...]).astype(BF16)

    acc = jnp.dot(an_ref[...], b_ref[...], preferred_element_type=F32)
    c, s1, s2 = c_ref[...], s1_ref[...], s2_ref[...]
    for h in range(heads_per_tile):
        lo = h * MLA_QK_PAD
        o_ref[:, lo:lo + LANES] = acc[:, lo:lo + LANES].astype(o_ref.dtype)
        o_ref[:, lo + LANES:lo + 2 * LANES] = _rope_slab(
            acc[:, lo + LANES:lo + 2 * LANES], c, s1, s2).astype(o_ref.dtype)


def _qup(lat, g, w, tabs, tm=1024, tn=1024):
    m = lat.shape[0]
    k = MLA_Q_RANK
    n = w.shape[1]
    nt = tabs[0].shape[0] // tm
    tab_spec = pl.BlockSpec((tm, LANES), lambda i, j: (i % nt, 0))
    return pl.pallas_call(
        functools.partial(_qup_kernel, heads_per_tile=tn // MLA_QK_PAD),
        grid=(m // tm, n // tn),
        in_specs=[pl.BlockSpec((tm, k), lambda i, j: (i, 0)),
                  pl.BlockSpec((1, k), lambda i, j: (0, 0)),
                  pl.BlockSpec((k, tn), lambda i, j: (0, j)),
                  tab_spec, tab_spec, tab_spec],
        out_specs=pl.BlockSpec((tm, tn), lambda i, j: (i, j)),
        out_shape=jax.ShapeDtypeStruct((m, n), BF16),
        scratch_shapes=[pltpu.VMEM((tm, k), BF16)],
        compiler_params=_params(("arbitrary", "arbitrary"), 48),
        name="mla_q_up",
    )(lat, g.reshape(1, k).astype(F32), w, *tabs)


def _kvup_kernel(a_ref, g_ref, b_ref, kr_ref, k_ref, v_ref):
    xn = _rms(a_ref[...], g_ref[...]).astype(BF16)
    acc = jnp.dot(xn, b_ref[...], preferred_element_type=F32)
    kr = kr_ref[...]
    for h in range(MLA_HEADS):
        lo = h * MLA_QK_PAD
        k_ref[:, lo:lo + LANES] = acc[:, h * LANES:(h + 1) * LANES].astype(k_ref.dtype)
        k_ref[:, lo + LANES:lo + 2 * LANES] = kr
    v_ref[...] = acc[:, MLA_HEADS * MLA_NOPE:].astype(v_ref.dtype)


def _kvup(lat, g, w, krope, tm=512):
    m = lat.shape[0]
    k = MLA_KV_RANK
    n = w.shape[1]
    lat_blk = MLA_Q_RANK // MLA_KV_RANK
    return pl.pallas_call(
        _kvup_kernel,
        grid=(m // tm,),
        in_specs=[pl.BlockSpec((tm, k), lambda i: (i, lat_blk)),
                  pl.BlockSpec((1, k), lambda i: (0, 0)),
                  pl.BlockSpec((k, n), lambda i: (0, 0)),
                  pl.BlockSpec((tm, LANES), lambda i: (i, 0))],
        out_specs=[pl.BlockSpec((tm, MLA_HEADS * MLA_QK_PAD), lambda i: (i, 0)),
                   pl.BlockSpec((tm, MLA_HEADS * MLA_V), lambda i: (i, 0))],
        out_shape=[jax.ShapeDtypeStruct((m, MLA_HEADS * MLA_QK_PAD), BF16),
                   jax.ShapeDtypeStruct((m, MLA_HEADS * MLA_V), BF16)],
        compiler_params=_params(("arbitrary",), 48),
        name="mla_kv_up",
    )(lat, g.reshape(1, k).astype(F32), w, krope)


def _tile_iotas(tk, tq):
    key = lax.broadcasted_iota(jnp.int32, (tk, tq), 0)
    qry = lax.broadcasted_iota(jnp.int32, (tk, tq), 1)
    return key, qry


def _chunk_visible(qry, key):
    shift = CHUNK.bit_length() - 1
    return jnp.right_shift(key, shift) <= jnp.right_shift(qry, shift)


def _stage_vt(v_ref, vt_ref, hp, dv):
    tk = vt_ref.shape[-1]
    for h in range(hp):
        for c in range(v_ref.shape[0] // tk):
            vt_ref[h, c] = v_ref[c * tk:(c + 1) * tk, h * dv:(h + 1) * dv].T


def _key_block(k_ref, j, tk, lo, width):
    start = pl.multiple_of(j * tk, tk)
    return k_ref[pl.ds(start, tk), lo:lo + width]


def _dot(a, b):
    return jnp.dot(a, b, preferred_element_type=F32)


def _colmax(x):
    return jnp.max(x, axis=0, keepdims=True)


def _colsum(x):
    return jnp.sum(x, axis=0, keepdims=True)


def _attn_call(kernel_fn, inputs, in_specs, d_out, dv, n_acc, hp, tq, tk, seq, heads, name):
    m = inputs[0].shape[0]
    nq = seq // tq
    return pl.pallas_call(
        kernel_fn,
        grid=(m // seq, heads // hp, nq),
        in_specs=in_specs,
        out_specs=pl.BlockSpec((tq, hp * d_out), lambda b, h, i: (b * nq + i, h)),
        out_shape=jax.ShapeDtypeStruct((m, heads * d_out), BF16),
        scratch_shapes=[pltpu.VMEM((hp, seq // tk, dv, tk), BF16),
                        pltpu.VMEM((n_acc * hp, dv, tq), F32)],
        compiler_params=_params(("arbitrary", "arbitrary", "arbitrary"), 56),
        name=name,
    )(*inputs)


def _mla_attn_kernel(q_ref, k_ref, v_ref, o_ref, vt_ref, acc_ref, *, hp):
    qi = pl.program_id(2)
    dq, dv = MLA_QK_PAD, MLA_V
    heads = range(hp)

    @pl.when(qi == 0)
    def _():
        _stage_vt(v_ref, vt_ref, hp, dv)

    c = LOG2E / math.sqrt(MLA_NOPE + MLA_ROPE)
    qt = [q_ref[:, h * dq:(h + 1) * dq].T for h in heads]
    tq, tk = q_ref.shape[0], vt_ref.shape[-1]
    nsub = tq // tk
    key, qry = _tile_iotas(tk, tq)
    base = qi * nsub

    def scores(j, key_offset=None):
        s = [_dot(_key_block(k_ref, j, tk, h *dq, dq), qt[h]) for h in heads]
        if key_offset is None:
            return s
        visible = _chunk_visible(qry, key + key_offset)
        return [jnp.where(visible, x, -1e30) for x in s]

    s = scores(base, 0)
    m = [_colmax(x) for x in s]
    p = [jnp.exp2((s[h] - m[h]) * c) for h in heads]
    l = [_colsum(x) for x in p]
    for h in heads:
        acc_ref[h] = _dot(vt_ref[h, base], p[h].astype(BF16))

    def update(s, j, m, l):
        m_new = [jnp.maximum(m[h], _colmax(s[h])) for h in heads]
        alpha = [jnp.exp2((m[h] - m_new[h]) * c) for h in heads]
        p = [jnp.exp2((s[h] - m_new[h]) * c) for h in heads]
        l = [alpha[h] * l[h] + _colsum(p[h]) for h in heads]
        pv = [_dot(vt_ref[h, j], p[h].astype(BF16)) for h in heads]
        for h in heads:
            acc_ref[h] = alpha[h] * acc_ref[h] + pv[h]
        return m_new, l

    for sub in range(1, nsub):
        m, l = update(scores(base + sub, sub * tk), base + sub, m, l)

    def body(j, carry):
        m, l = update(scores(j), j, carry[:hp], carry[hp:])
        return tuple(m) + tuple(l)

    carry = lax.fori_loop(0, base, body, tuple(m) + tuple(l))
    l = carry[hp:]
    for h in heads:
        o_ref[:, h * dv:(h + 1) * dv] = (acc_ref[h] / l[h]).T.astype(o_ref.dtype)


def _mla_attention(q, k, v, seq, hp=4, tq=512, tk=512):
    nq = seq // tq
    in_specs = [pl.BlockSpec((tq, hp * MLA_QK_PAD), lambda b, h, i: (b * nq + i, h)),
                pl.BlockSpec((seq, hp * MLA_QK_PAD), lambda b, h, i: (b, h)),
                pl.BlockSpec((seq, hp * MLA_V), lambda b, h, i: (b, h))]
    return _attn_call(functools.partial(_mla_attn_kernel, hp=hp), (q, k, v), in_specs,
                      MLA_V, MLA_V, 1, hp, tq, tk, seq, MLA_HEADS, "mla_attention")


def _sb_attn_kernel(q_ref, k_ref, v_ref, o_ref, vt_ref, acc_ref, *, hp):
    qi = pl.program_id(2)
    d = SB_DIM
    heads = range(hp)

    @pl.when(qi == 0)
    def _():
        _stage_vt(v_ref, vt_ref, hp, d)

    scale = 1.0 / math.sqrt(SB_DIM)
    qt = [q_ref[:, h * d:(h + 1) * d].T for h in heads]
    tq, tk = q_ref.shape[0], vt_ref.shape[-1]
    nsub = tq // tk
    key, qry = _tile_iotas(tk, tq)
    base = qi * nsub
    later = (lax.broadcasted_iota(jnp.int32, (tk, tk), 1)
             > lax.broadcasted_iota(jnp.int32, (tk, tk), 0)).astype(BF16)
    later2 = jnp.concatenate([later, later], axis=1)

    def block(j, r, key_offset=None):
        diagonal = key_offset is not None
        if diagonal:
            strict = key + key_offset < qry
        z2 = [_dot(_key_block(k_ref, j, tk, h *d, d), qt[h]) * (scale * LOG2E) for h in heads]
        nz2 = [-x for x in z2]
        t = [jnp.log2(1.0 + jnp.exp2(jnp.minimum(z2[h], nz2[h]))) for h in heads]
        log_beta = [jnp.minimum(z2[h], 0.0) - t[h] for h in heads]
        neg_keep = [jnp.maximum(z2[h], 0.0) + t[h] for h in heads]
        if diagonal:
            neg_keep = [jnp.where(strict, x, 0.0) for x in neg_keep]
        hi = [x.astype(BF16) for x in neg_keep]
        lo = [(neg_keep[h] - hi[h].astype(F32)).astype(BF16) for h in heads]
        after = [_dot(later2, jnp.concatenate([hi[h], lo[h]], axis=0)) for h in heads]
        a = [jnp.exp2(log_beta[h] - after[h] - r[h]) for h in heads]
        if diagonal:
            a = [jnp.where(strict, x, 0.0) for x in a]
        pv = [_dot(vt_ref[h, j], a[h].astype(BF16)) for h in heads]
        r = [r[h] + after[h][0:1, :] + neg_keep[h][0:1, :] for h in heads]
        return r, pv

    r = [jnp.zeros((1, tq), F32)] * hp
    for sub in reversed(range(nsub)):
        r, pv = block(base + sub, r, sub * tk)
        for h in heads:
            if sub == nsub - 1:
                acc_ref[h] = pv[h]
            else:
                acc_ref[h] += pv[h]

    def body(it, r):
        r, pv = block(base - 1 - it, list(r))
        for h in heads:
            acc_ref[h] += pv[h]
        return tuple(r)

    lax.fori_loop(0, base, body, tuple(r))
    for h in heads:
        o_ref[:, h * d:(h + 1) * d] = acc_ref[h].T.astype(o_ref.dtype)


def _sb_attention(qkv, seq, hp=4, tq=512, tk=256):
    nq = seq // tq
    groups = SB_HEADS // hp
    in_specs = [pl.BlockSpec((tq, hp *SB_DIM), lambda b, h, i: (b * nq + i, h)),
                pl.BlockSpec((seq, hp * SB_DIM), lambda b, h, i: (b, groups + h)),
                pl.BlockSpec((seq, hp * SB_DIM), lambda b, h, i: (b, 2 * groups + h))]
    return _attn_call(functools.partial(_sb_attn_kernel, hp=hp), (qkv, qkv, qkv), in_specs,
                      SB_DIM, SB_DIM, 1, hp, tq, tk, seq, SB_HEADS, "sb_attention")


def _diff_attn_kernel(q_ref, k_ref, v_ref, slope_ref, lq1_ref, lk1_ref, lq2_ref, lk2_ref,
                      g_ref, o_ref, vt_ref, acc_ref, *, lambda_init, hp):
    qi = pl.program_id(2)
    d = DIFF_DIM
    dv = 2 * d
    maps = range(2 * hp)

    @pl.when(qi == 0)
    def _():
        _stage_vt(v_ref, vt_ref, hp, dv)

    inv_scale = math.sqrt(DIFF_DIM)
    c2 = LOG2E / inv_scale
    qt = [q_ref[:, c * d:(c + 1) * d].T for c in maps]
    slope = [slope_ref[c // 2, :, 0:1] * inv_scale for c in maps]
    tq, tk = q_ref.shape[0], vt_ref.shape[-1]
    nsub = tq // tk
    key, qry = _tile_iotas(tk, tq)
    base = qi * nsub
    rel = (qry - key).astype(F32)
    bias_off = [-slope[2 * h] * rel for h in range(hp)]

    def scores(j):
        return [_dot(_key_block(k_ref, j, tk, c * d, d), qt[c]) for c in maps]

    def diag_scores(sub):
        s = scores(base + sub)
        visible = _chunk_visible(qry, key + sub * tk)
        dist = jnp.abs(rel - float(sub * tk))
        return [jnp.where(visible, s[c] - slope[c] * dist, -1e30) for c in maps]

    s = diag_scores(0)
    m = [_colmax(x) for x in s]
    p = [jnp.exp2((s[c] - m[c]) * c2) for c in maps]
    l = [_colsum(x) for x in p]
    for c in maps:
        acc_ref[c] = _dot(vt_ref[c // 2, base], p[c].astype(BF16))

    def update(u, off, j, m, l):
        m_new = [jnp.maximum(m[c], _colmax(u[c]) + off[c]) for c in maps]
        alpha = [jnp.exp2((m[c] - m_new[c]) * c2) for c in maps]
        p = [jnp.exp2((u[c] - (m_new[c] - off[c])) * c2) for c in maps]
        l = [alpha[c] * l[c] + _colsum(p[c]) for c in maps]
        pv = [_dot(vt_ref[c // 2, j], p[c].astype(BF16)) for c in maps]
        for c in maps:
            acc_ref[c] = alpha[c] * acc_ref[c] + pv[c]
        return m_new, l

    zero = jnp.zeros((1, 1), F32)
    for sub in range(1, nsub):
        m, l = update(diag_scores(sub), [zero] * (2 * hp), base + sub, m, l)

    def body(it, carry):
        j = base - 1 - it
        dist = ((base - j) * tk).astype(F32)
        s = scores(j)
        u = [s[c] + bias_off[c // 2] for c in maps]
        m, l = update(u, [-slope[c] * dist for c in maps], j, carry[:2 * hp], carry[2 * hp:])
        return tuple(m) + tuple(l)

    carry = lax.fori_loop(0, base, body, tuple(m) + tuple(l))
    l = carry[2 * hp:]

    lam = (jnp.exp(jnp.sum(lq1_ref[...] * lk1_ref[...], axis=-1, keepdims=True))
           - jnp.exp(jnp.sum(lq2_ref[...] * lk2_ref[...], axis=-1, keepdims=True))
           + lambda_init)
    for h in range(hp):
        ot = acc_ref[2 * h] / l[2 * h] - lam * (acc_ref[2 * h + 1] / l[2 * h + 1])
        yt = ot * lax.rsqrt(jnp.mean(ot * ot, axis=0, keepdims=True) + EPS)
        o_ref[:, h * dv:(h + 1) * dv] = (yt.T * g_ref[...] * (1.0 - lambda_init)).astype(o_ref.dtype)


def _diff_attention(qkv, slopes, lq1, lk1, lq2, lk2, subln, lambda_init, seq, hp=2, tq=512, tk=512):
    nq = seq // tq
    dh = 2 * DIFF_DIM
    groups = DIFF_HEADS // hp
    vec = pl.BlockSpec((1, DIFF_DIM), lambda b, h, i: (0, 0))
    in_specs = [pl.BlockSpec((tq, hp *dh), lambda b, h, i: (b * nq + i, h)),
                pl.BlockSpec((seq, hp * dh), lambda b, h, i: (b, groups + h)),
                pl.BlockSpec((seq, hp * dh), lambda b, h, i: (b, 2 * groups + h)),
                pl.BlockSpec((hp, 1, LANES), lambda b, h, i: (h, 0, 0)),
                vec, vec, vec, vec,
                pl.BlockSpec((1, dh), lambda b, h, i: (0, 0))]
    inputs = (qkv, qkv, qkv, slopes,
              lq1.reshape(1, -1).astype(F32), lk1.reshape(1, -1).astype(F32),
              lq2.reshape(1, -1).astype(F32), lk2.reshape(1, -1).astype(F32),
              subln.reshape(1, dh).astype(F32))
    return _attn_call(functools.partial(_diff_attn_kernel, lambda_init=lambda_init, hp=hp),
                      inputs, in_specs, dh, dh, 2, hp, tq, tk, seq, DIFF_HEADS, "diff_attention")


def _out_proj_kernel(a1_ref, a2_ref, b_ref, r_ref, o_ref):
    half = a1_ref.shape[1]
    acc = jnp.dot(a1_ref[...], b_ref[:half, :], preferred_element_type=F32)
    acc = acc + jnp.dot(a2_ref[...], b_ref[half:, :], preferred_element_type=F32)
    o_ref[...] = r_ref[...] + acc


def _out_proj(a1, a2, blk1, blk2, b, res, tm=1024, tn=512):
    m = res.shape[0]
    k, n = b.shape
    half = k // 2
    return pl.pallas_call(
        _out_proj_kernel,
        grid=(m // tm, n // tn),
        in_specs=[pl.BlockSpec((tm, half), lambda i, j: (i, blk1)),
                  pl.BlockSpec((tm, half), lambda i, j: (i, blk2)),
                  pl.BlockSpec((k, tn), lambda i, j: (0, j)),
                  pl.BlockSpec((tm, tn), lambda i, j: (i, j))],
        out_specs=pl.BlockSpec((tm, tn), lambda i, j: (i, j)),
        out_shape=jax.ShapeDtypeStruct((m, n), F32),
        compiler_params=_params(("arbitrary", "arbitrary"), 48),
        name="out_proj",
    )(a1, a2, b, res)


def _ffn_up_kernel(a_ref, bg_ref, bu_ref, cwg_ref, cwu_ref, cbg_ref, cbu_ref, o_ref,
                   h_ref, carry_ref, *, tm, tiles_per_seq):
    i = pl.program_id(0)
    j = pl.program_id(1)
    halo = FFN_HALO
    width = 2 * FFN_TN

    @pl.when(i % tiles_per_seq == 0)
    def _():
        h_ref[0:halo, :] = jnp.zeros((halo, width), F32)

    @pl.when(i % tiles_per_seq != 0)
    def _():
        h_ref[0:halo, :] = carry_ref[j]

    a = a_ref[...]
    h_ref[halo:halo + tm, 0:FFN_TN] = _dot(a, bg_ref[...].astype(BF16))
    h_ref[halo:halo + tm, FFN_TN:width] = _dot(a, bu_ref[...].astype(BF16))
    carry_ref[j] = h_ref[tm:tm + halo, :]

    wg, wu = cwg_ref[...], cwu_ref[...]
    w = [jnp.concatenate([wg[t:t + 1], wu[t:t + 1]], axis=1) for t in range(CONV_W)]
    bias = jnp.concatenate([cbg_ref[...], cbu_ref[...]], axis=1)
    rows = 128
    for c in range(tm // rows):
        x = h_ref[c * rows:c * rows + halo + rows, :]
        conv = bias
        for t in range(CONV_W):
            back = CONV_W - 1 - t
            shifted = pltpu.roll(x, back, 0) if back else x
            conv = conv + shifted[halo:, :] * w[t]
        g = conv[:, :FFN_TN]
        u = conv[:, FFN_TN:]
        o_ref[c * rows:(c + 1) * rows, :] = (g * jax.nn.sigmoid(g) * u).astype(o_ref.dtype)


def _ffn_up(a, w_up, conv_w, conv_b, layer, seq, tm=1024):
    m, k = a.shape
    nj = FFN_DIM // FFN_TN
    width = 2 * FFN_TN
    return pl.pallas_call(
        functools.partial(_ffn_up_kernel, tm=tm, tiles_per_seq=seq // tm),
        grid=(m // tm, nj),
        in_specs=[pl.BlockSpec((tm, k), lambda i, j: (i, 0)),
                  pl.BlockSpec((None, k, FFN_TN), lambda i, j: (layer, 0, j)),
                  pl.BlockSpec((None, k, FFN_TN), lambda i, j: (layer, 0, nj + j)),
                  pl.BlockSpec((None, CONV_W, FFN_TN), lambda i, j: (layer, 0, j)),
                  pl.BlockSpec((None, CONV_W, FFN_TN), lambda i, j: (layer, 0, nj + j)),
                  pl.BlockSpec((None, 1, FFN_TN), lambda i, j: (layer, 0, j)),
                  pl.BlockSpec((None, 1, FFN_TN), lambda i, j: (layer, 0, nj + j))],
        out_specs=pl.BlockSpec((tm, FFN_TN), lambda i, j: (i, j)),
        out_shape=jax.ShapeDtypeStruct((m, FFN_DIM), BF16),
        scratch_shapes=[pltpu.VMEM((tm + FFN_HALO, width), F32),
                        pltpu.VMEM((nj, FFN_HALO, width), F32)],
        compiler_params=_params(("arbitrary", "arbitrary"), 48),
        name="ffn_up_conv_gate",
    )(a, w_up, w_up, conv_w, conv_w, conv_b, conv_b)


def _ffn_down_kernel(a_ref, b_ref, r_ref, o_ref):
    o_ref[...] = r_ref[...] + _dot(a_ref[...], b_ref[...].astype(BF16))


def _ffn_down(a, b, res, layer, tm=1024, tn=256):
    m, k = a.shape
    n = b.shape[2]
    return pl.pallas_call(
        _ffn_down_kernel,
        grid=(m // tm, n // tn),
        in_specs=[pl.BlockSpec((tm, k), lambda i, j: (i, 0), pipeline_mode=pl.Buffered(1)),
                  pl.BlockSpec((None, k, tn), lambda i, j: (layer, 0, j)),
                  pl.BlockSpec((tm, tn), lambda i, j: (i, j))],
        out_specs=pl.BlockSpec((tm, tn), lambda i, j: (i, j)),
        out_shape=jax.ShapeDtypeStruct((m, n), F32),
        compiler_params=_params(("arbitrary", "arbitrary"), 56),
        name="ffn_down",
    )(a, b, res)


def _rope_tables(seq):
    half = MLA_ROPE // 2
    inv = ROPE_THETA ** (-jnp.arange(half, dtype=F32) / half)
    ang = jnp.arange(seq).astype(F32)[:, None] * inv[None, :]
    cos, sin = jnp.cos(ang), jnp.sin(ang)
    zero = jnp.zeros_like(cos)
    pad = jnp.zeros((seq, LANES - MLA_ROPE), F32)
    c = jnp.concatenate([cos, cos, pad], axis=1)
    s1 = jnp.concatenate([-sin, zero, pad], axis=1)
    s2 = jnp.concatenate([zero, sin, pad], axis=1)
    return c, s1, s2


def _conv_ffn(h, hn, w_up, conv_w, conv_b, w_down, layer, seq):
    act = _ffn_up(hn, w_up, conv_w, conv_b, layer, seq)
    return _ffn_down(act, w_down, h, layer)


def _mla_sb_layer(h, hn, w_in, q_norm, w_uq, kv_norm, w_ukv, w_out, tabs, seq):
    o1 = MLA_Q_RANK + MLA_KV_RANK
    o2 = o1 + MLA_ROPE
    w_lat = w_in[:, :o1].astype(BF16)
    w_kr = jnp.pad(w_in[:, o1:o2], ((0, 0), (0, LANES - MLA_ROPE))).astype(BF16)
    w_sb = w_in[:, o2:].astype(BF16)
    w_q = jnp.pad(w_uq.reshape(MLA_Q_RANK, MLA_HEADS, MLA_NOPE + MLA_ROPE),
                  ((0, 0), (0, 0), (0, MLA_QK_PAD - MLA_NOPE - MLA_ROPE)))
    w_q = w_q.reshape(MLA_Q_RANK, MLA_HEADS * MLA_QK_PAD).astype(BF16)
    w_kv = w_ukv.reshape(MLA_KV_RANK, MLA_HEADS, 2, MLA_NOPE).transpose(0, 2, 1, 3)
    w_kv = w_kv.reshape(MLA_KV_RANK, 2 * MLA_HEADS * MLA_NOPE).astype(BF16)

    lat = _matmul(hn, w_lat, F32, 1024, 768, "mla_latent_proj")
    krope = _krope(hn, w_kr, tabs)
    sb = _matmul(hn, w_sb, BF16, 1024, 1024, "sb_qkv_proj")
    q = _qup(lat, q_norm, w_q, tabs)
    k, v = _kvup(lat, kv_norm, w_kv, krope)
    o_a = _mla_attention(q, k, v, seq)
    o_b = _sb_attention(sb, seq)
    return _out_proj(o_a, o_b, 0, 0, w_out.astype(BF16), h)


def _diff_layer(h, hn, w_in, lq1, lk1, lq2, lk2, subln, w_out, layer_idx, seq):
    lambda_init = 0.8 - 0.6 * math.exp(-0.3 * layer_idx)
    slopes = jnp.asarray([2.0 ** (-8.0 * (i + 1) / DIFF_HEADS) for i in range(DIFF_HEADS)], F32)
    slopes = jnp.broadcast_to(slopes[:, None, None], (DIFF_HEADS, 1, LANES))
    qkv = _matmul(hn, w_in, BF16, 2048, 512, "diff_qkv_proj", a_buffers=1)
    o = _diff_attention(qkv, slopes, lq1, lk1, lq2, lk2, subln, lambda_init, seq)
    return _out_proj(o, o, 0, 1, w_out.astype(BF16), h)


def kernel(x, norm_mix, norm_ffn, norm_final, ab_w_in, ab_q_norm, ab_w_uq, ab_kv_norm,
           ab_w_ukv, ab_w_out, c_w_in, c_lambda_q1, c_lambda_k1, c_lambda_q2, c_lambda_k2,
           c_subln, c_w_out, ffn_w_up, ffn_conv_w, ffn_conv_b, ffn_w_down):
    batch, seq, d = x.shape
    depth = norm_mix.shape[0]
    tabs = _rope_tables(seq)
    w_up = ffn_w_up
    w_down = ffn_w_down
    conv_w = ffn_conv_w.astype(F32)
    conv_b = ffn_conv_b.astype(F32).reshape(depth, 1, -1)
    h = x.reshape(batch * seq, d)
    for i in range(depth):
        j = i // 2
        hn = _rmsnorm(h, norm_mix[i], BF16)
        if i % 2 == 0:
            h = _mla_sb_layer(h, hn, ab_w_in[j], ab_q_norm[j], ab_w_uq[j], ab_kv_norm[j],
                              ab_w_ukv[j], ab_w_out[j], tabs, seq)
        else:
            h = _diff_layer(h, hn, c_w_in[j], c_lambda_q1[j], c_lambda_k1[j], c_lambda_q2[j],
                            c_lambda_k2[j], c_subln[j], c_w_out[j], i, seq)
        hn = _rmsnorm(h, norm_ffn[i], BF16)
        h = _conv_ffn(h, hn, w_up, conv_w, conv_b, w_down, i, seq)
    return _rmsnorm(h, norm_final, x.dtype).reshape(batch, seq, d)
```

```python
import functools
import math

import jax
import jax.numpy as jnp
from jax import lax
from jax.experimental import pallas as pl
from jax.experimental.pallas import tpu as pltpu

F32 = jnp.float32
BF16 = jnp.bfloat16

CHUNK = 64
EPS = 1e-6
ROPE_THETA = 10000.0
MLA_HEADS = 16
MLA_Q_RANK = 1024
MLA_KV_RANK = 512
MLA_NOPE = 128
MLA_ROPE = 64
MLA_V = 128
SB_HEADS = 16
SB_DIM = 128
DIFF_HEADS = 16
DIFF_DIM = 128
FFN_DIM = 11008
CONV_W = 3

LANES = 128
MLA_QK_PAD = 256
FFN_TN = 256
FFN_HALO = 8
SB_CUM = 256
LOG2E = math.log2(math.e)
MIB = 1024 * 1024


def _params(sem, vmem_mib):
    return pltpu.CompilerParams(dimension_semantics=sem, vmem_limit_bytes=vmem_mib * MIB)


def _rms(x, g):
    return x * lax.rsqrt(jnp.mean(x * x, axis=-1, keepdims=True) + EPS) * g


def _rope_slab(x, c, s1, s2):
    return x * c + pltpu.roll(x, 96, 1) * s1 + pltpu.roll(x, 32, 1) * s2


def _rmsnorm_kernel(x_ref, g_ref, o_ref):
    o_ref[...] = _rms(x_ref[...].astype(F32), g_ref[...]).astype(o_ref.dtype)


def _rmsnorm(x, g, out_dtype, tm=512):
    m, d = x.shape
    return pl.pallas_call(
        _rmsnorm_kernel,
        grid=(m // tm,),
        in_specs=[pl.BlockSpec((tm, d), lambda i: (i, 0)),
                  pl.BlockSpec((1, d), lambda i: (0, 0))],
        out_specs=pl.BlockSpec((tm, d), lambda i: (i, 0)),
        out_shape=jax.ShapeDtypeStruct((m, d), out_dtype),
        compiler_params=_params(("arbitrary",), 48),
        name="rmsnorm",
    )(x, g.reshape(1, d).astype(F32))


def _mm_kernel(a_ref, b_ref, o_ref):
    o_ref[...] = jnp.dot(a_ref[...], b_ref[...].astype(BF16),
                         preferred_element_type=F32).astype(o_ref.dtype)


def _matmul(a, b, out_dtype, tm, tn, name, a_buffers=2):
    m, k = a.shape
    n = b.shape[1]
    return pl.pallas_call(
        _mm_kernel,
        grid=(m // tm, n // tn),
        in_specs=[pl.BlockSpec((tm, k), lambda i, j: (i, 0),
                               pipeline_mode=pl.Buffered(a_buffers)),
                  pl.BlockSpec((k, tn), lambda i, j: (0, j))],
        out_specs=pl.BlockSpec((tm, tn), lambda i, j: (i, j)),
        out_shape=jax.ShapeDtypeStruct((m, n), out_dtype),
        compiler_params=_params(("arbitrary", "arbitrary"), 56),
        name=name,
    )(a, b)


def _krope_kernel(a_ref, b_ref, c_ref, s1_ref, s2_ref, o_ref):
    acc = jnp.dot(a_ref[...], b_ref[...], preferred_element_type=F32)
    o_ref[...] = _rope_slab(acc, c_ref[...], s1_ref[...], s2_ref[...]).astype(o_ref.dtype)


def _krope(a, b, tabs, tm=1024):
    m, k = a.shape
    nt = tabs[0].shape[0] // tm
    tab_spec = pl.BlockSpec((tm, LANES), lambda i: (i % nt, 0))
    return pl.pallas_call(
        _krope_kernel,
        grid=(m // tm,),
        in_specs=[pl.BlockSpec((tm, k), lambda i: (i, 0)),
                  pl.BlockSpec((k, LANES), lambda i: (0, 0)),
                  tab_spec, tab_spec, tab_spec],
        out_specs=pl.BlockSpec((tm, LANES), lambda i: (i, 0)),
        out_shape=jax.ShapeDtypeStruct((m, LANES), BF16),
        compiler_params=_params(("arbitrary",), 40),
        name="krope_proj",
    )(a, b, *tabs)


def _qup_kernel(a_ref, g_ref, b_ref, c_ref, s1_ref, s2_ref, o_ref, an_ref, *, heads_per_tile):
    @pl.when(pl.program_id(1) == 0)
    def _():
        an_ref[...] = _rms(a_ref[...], g_ref[...]).astype(BF16)

    acc = jnp.dot(an_ref[...], b_ref[...], preferred_element_type=F32)
    c, s1, s2 = c_ref[...], s1_ref[...], s2_ref[...]
    for h in range(heads_per_tile):
        lo = h * MLA_QK_PAD
        o_ref[:, lo:lo + LANES] = acc[:, lo:lo + LANES].astype(o_ref.dtype)
        o_ref[:, lo + LANES:lo + 2 * LANES] = _rope_slab(
            acc[:, lo + LANES:lo + 2 * LANES], c, s1, s2).astype(o_ref.dtype)


def _qup(lat, g, w, tabs, tm=1024, tn=1024):
    m = lat.shape[0]
    k = MLA_Q_RANK
    n = w.shape[1]
    nt = tabs[0].shape[0] // tm
    tab_spec = pl.BlockSpec((tm, LANES), lambda i, j: (i % nt, 0))
    return pl.pallas_call(
        functools.partial(_qup_kernel, heads_per_tile=tn // MLA_QK_PAD),
        grid=(m // tm, n // tn),
        in_specs=[pl.BlockSpec((tm, k), lambda i, j: (i, 0)),
                  pl.BlockSpec((1, k), lambda i, j: (0, 0)),
                  pl.BlockSpec((k, tn), lambda i, j: (0, j)),
                  tab_spec, tab_spec, tab_spec],
        out_specs=pl.BlockSpec((tm, tn), lambda i, j: (i, j)),
        out_shape=jax.ShapeDtypeStruct((m, n), BF16),
        scratch_shapes=[pltpu.VMEM((tm, k), BF16)],
        compiler_params=_params(("arbitrary", "arbitrary"), 48),
        name="mla_q_up",
    )(lat, g.reshape(1, k).astype(F32), w, *tabs)


def _kvup_kernel(a_ref, g_ref, b_ref, kr_ref, k_ref, v_ref):
    xn = _rms(a_ref[...], g_ref[...]).astype(BF16)
    acc = jnp.dot(xn, b_ref[...], preferred_element_type=F32)
    kr = kr_ref[...]
    for h in range(MLA_HEADS):
        lo = h * MLA_QK_PAD
        k_ref[:, lo:lo + LANES] = acc[:, h * LANES:(h + 1) * LANES].astype(k_ref.dtype)
        k_ref[:, lo + LANES:lo + 2 * LANES] = kr
    v_ref[...] = acc[:, MLA_HEADS * MLA_NOPE:].astype(v_ref.dtype)


def _kvup(lat, g, w, krope, tm=512):
    m = lat.shape[0]
    k = MLA_KV_RANK
    n = w.shape[1]
    lat_blk = MLA_Q_RANK // MLA_KV_RANK
    return pl.pallas_call(
        _kvup_kernel,
        grid=(m // tm,),
        in_specs=[pl.BlockSpec((tm, k), lambda i: (i, lat_blk)),
                  pl.BlockSpec((1, k), lambda i: (0, 0)),
                  pl.BlockSpec((k, n), lambda i: (0, 0)),
                  pl.BlockSpec((tm, LANES), lambda i: (i, 0))],
        out_specs=[pl.BlockSpec((tm, MLA_HEADS * MLA_QK_PAD), lambda i: (i, 0)),
                   pl.BlockSpec((tm, MLA_HEADS * MLA_V), lambda i: (i, 0))],
        out_shape=[jax.ShapeDtypeStruct((m, MLA_HEADS * MLA_QK_PAD), BF16),
                   jax.ShapeDtypeStruct((m, MLA_HEADS * MLA_V), BF16)],
        compiler_params=_params(("arbitrary",), 48),
        name="mla_kv_up",
    )(lat, g.reshape(1, k).astype(F32), w, krope)


def _tile_iotas(tk, tq):
    key = lax.broadcasted_iota(jnp.int32, (tk, tq), 0)
    qry = lax.broadcasted_iota(jnp.int32, (tk, tq), 1)
    return key, qry


def _chunk_visible(qry, key):
    shift = CHUNK.bit_length() - 1
    return jnp.right_shift(key, shift) <= jnp.right_shift(qry, shift)


def _stage_vt(v_ref, vt_ref, hp, dv):
    tk = vt_ref.shape[-1]
    for h in range(hp):
        for c in range(v_ref.shape[0] // tk):
            vt_ref[h, c] = v_ref[c * tk:(c + 1) * tk, h * dv:(h + 1) * dv].T


def _key_block(k_ref, j, tk, lo, width):
    start = pl.multiple_of(j * tk, tk)
    return k_ref[pl.ds(start, tk), lo:lo + width]


def _dot(a, b):
    return jnp.dot(a, b, preferred_element_type=F32)


def _neg_abs(x):
    bits = lax.bitcast_convert_type(x, jnp.uint32) | jnp.uint32(0x80000000)
    return lax.bitcast_convert_type(bits, F32)


def _colmax(x):
    return jnp.max(x, axis=0, keepdims=True)


def _colsum(x):
    return jnp.sum(x, axis=0, keepdims=True)


def _attn_call(kernel_fn, inputs, in_specs, d_out, dv, n_acc, hp, tq, tk, seq, heads, name):
    m = inputs[0].shape[0]
    nq = seq // tq
    return pl.pallas_call(
        kernel_fn,
        grid=(m // seq, heads // hp, nq),
        in_specs=in_specs,
        out_specs=pl.BlockSpec((tq, hp * d_out), lambda b, h, i: (b * nq + i, h)),
        out_shape=jax.ShapeDtypeStruct((m, heads * d_out), BF16),
        scratch_shapes=[pltpu.VMEM((hp, seq // tk, dv, tk), BF16),
                        pltpu.VMEM((n_acc * hp, dv, tq), F32)],
        compiler_params=_params(("arbitrary", "arbitrary", "arbitrary"), 56),
        name=name,
    )(*inputs)


def _mla_attn_kernel(q_ref, k_ref, v_ref, o_ref, vt_ref, acc_ref, *, hp):
    qi = pl.program_id(2)
    dq, dv = MLA_QK_PAD, MLA_V
    heads = range(hp)

    @pl.when(qi == 0)
    def _():
        _stage_vt(v_ref, vt_ref, hp, dv)

    c = LOG2E / math.sqrt(MLA_NOPE + MLA_ROPE)
    qt = [q_ref[:, h * dq:(h + 1) * dq].T for h in heads]
    tq, tk = q_ref.shape[0], vt_ref.shape[-1]
    nsub = tq // tk
    key, qry = _tile_iotas(tk, tq)
    base = qi * nsub

    def scores(j, key_offset=None):
        s = [_dot(_key_block(k_ref, j, tk, h *dq, dq), qt[h]) for h in heads]
        if key_offset is None:
            return s
        visible = _chunk_visible(qry, key + key_offset)
        return [jnp.where(visible, x, -1e30) for x in s]

    s = scores(base, 0)
    m = [_colmax(x) for x in s]
    p = [jnp.exp2((s[h] - m[h]) * c) for h in heads]
    l = [_colsum(x) for x in p]
    for h in heads:
        acc_ref[h] = _dot(vt_ref[h, base], p[h].astype(BF16))

    def update(s, j, m, l):
        m_new = [jnp.maximum(m[h], _colmax(s[h])) for h in heads]
        alpha = [jnp.exp2((m[h] - m_new[h]) * c) for h in heads]
        p = [jnp.exp2((s[h] - m_new[h]) * c) for h in heads]
        l = [alpha[h] * l[h] + _colsum(p[h]) for h in heads]
        pv = [_dot(vt_ref[h, j], p[h].astype(BF16)) for h in heads]
        for h in heads:
            acc_ref[h] = alpha[h] * acc_ref[h] + pv[h]
        return m_new, l

    for sub in range(1, nsub):
        m, l = update(scores(base + sub, sub * tk), base + sub, m, l)

    def body(j, carry):
        m, l = update(scores(j), j, carry[:hp], carry[hp:])
        return tuple(m) + tuple(l)

    carry = lax.fori_loop(0, base, body, tuple(m) + tuple(l))
    l = carry[hp:]
    for h in heads:
        o_ref[:, h * dv:(h + 1) * dv] = (acc_ref[h] / l[h]).T.astype(o_ref.dtype)


def _mla_attention(q, k, v, seq, hp=4, tq=512, tk=512):
    nq = seq // tq
    in_specs = [pl.BlockSpec((tq, hp * MLA_QK_PAD), lambda b, h, i: (b * nq + i, h)),
                pl.BlockSpec((seq, hp * MLA_QK_PAD), lambda b, h, i: (b, h)),
                pl.BlockSpec((seq, hp * MLA_V), lambda b, h, i: (b, h))]
    return _attn_call(functools.partial(_mla_attn_kernel, hp=hp), (q, k, v), in_specs,
                      MLA_V, MLA_V, 1, hp, tq, tk, seq, MLA_HEADS, "mla_attention")


def _sb_attn_kernel(q_ref, k_ref, v_ref, o_ref, vt_ref, acc_ref, *, hp):
    qi = pl.program_id(2)
    d = SB_DIM
    heads = range(hp)

    @pl.when(qi == 0)
    def _():
        _stage_vt(v_ref, vt_ref, hp, d)

    scale = 1.0 / math.sqrt(SB_DIM)
    qt = [q_ref[:, h * d:(h + 1) * d].T for h in heads]
    tq, tk = q_ref.shape[0], vt_ref.shape[-1]
    nsub = tq // tk
    key, qry = _tile_iotas(tk, tq)
    base = qi * nsub
    later = (lax.broadcasted_iota(jnp.int32, (SB_CUM, SB_CUM), 1)
             > lax.broadcasted_iota(jnp.int32, (SB_CUM, SB_CUM), 0)).astype(BF16)
    later2 = jnp.concatenate([later, later], axis=1)

    def cumsum_later(x):
        pieces, total = [], None
        for c in reversed(range(tk // SB_CUM)):
            xc = x[c * SB_CUM:(c + 1) * SB_CUM, :]
            hi = xc.astype(BF16)
            lo = (xc - hi.astype(F32)).astype(BF16)
            local = _dot(later2, jnp.concatenate([hi, lo], axis=0))
            if total is not None:
                local = local + total
            total = local[0:1, :] + xc[0:1, :]
            pieces.append(local)
        return pieces[0] if len(pieces) == 1 else jnp.concatenate(pieces[::-1], axis=0)

    def block(j, r, key_offset=None):
        diagonal = key_offset is not None
        if diagonal:
            strict = key + key_offset < qry
        z2 = [_dot(_key_block(k_ref, j, tk, h * d, d), qt[h]) * (scale * LOG2E) for h in heads]
        t = [jnp.log2(1.0 + jnp.exp2(_neg_abs(x))) for x in z2]
        neg_keep = [jnp.maximum(z2[h], 0.0) + t[h] for h in heads]
        log_beta = [z2[h] - neg_keep[h] for h in heads]
        if diagonal:
            neg_keep = [jnp.where(strict, x, 0.0) for x in neg_keep]
        after = [cumsum_later(x) for x in neg_keep]
        a = [jnp.exp2(log_beta[h] - after[h] - r[h]) for h in heads]
        if diagonal:
            a = [jnp.where(strict, x, 0.0) for x in a]
        pv = [_dot(vt_ref[h, j], a[h].astype(BF16)) for h in heads]
        r = [r[h] + after[h][0:1, :] + neg_keep[h][0:1, :] for h in heads]
        return r, pv

    r = [jnp.zeros((1, tq), F32)] * hp
    for sub in reversed(range(nsub)):
        r, pv = block(base + sub, r, sub * tk)
        for h in heads:
            if sub == nsub - 1:
                acc_ref[h] = pv[h]
            else:
                acc_ref[h] += pv[h]

    def body(it, r):
        r, pv = block(base - 1 - it, list(r))
        for h in heads:
            acc_ref[h] += pv[h]
        return tuple(r)

    lax.fori_loop(0, base, body, tuple(r))
    for h in heads:
        o_ref[:, h * d:(h + 1) * d] = acc_ref[h].T.astype(o_ref.dtype)


def _sb_attention(qkv, seq, hp=4, tq=512, tk=512):
    nq = seq // tq
    groups = SB_HEADS // hp
    in_specs = [pl.BlockSpec((tq, hp * SB_DIM), lambda b, h, i: (b * nq + i, h)),
                pl.BlockSpec((seq, hp * SB_DIM), lambda b, h, i: (b, groups + h)),
                pl.BlockSpec((seq, hp * SB_DIM), lambda b, h, i: (b, 2 * groups + h))]
    return _attn_call(functools.partial(_sb_attn_kernel, hp=hp), (qkv, qkv, qkv), in_specs,
                      SB_DIM, SB_DIM, 1, hp, tq, tk, seq, SB_HEADS, "sb_attention")


def _diff_attn_kernel(q_ref, k_ref, v_ref, slope_ref, lq1_ref, lk1_ref, lq2_ref, lk2_ref,
                      g_ref, o_ref, vt_ref, acc_ref, *, lambda_init, hp):
    qi = pl.program_id(2)
    d = DIFF_DIM
    dv = 2 * d
    maps = range(2 * hp)

    @pl.when(qi == 0)
    def _():
        _stage_vt(v_ref, vt_ref, hp, dv)

    inv_scale = math.sqrt(DIFF_DIM)
    c2 = LOG2E / inv_scale
    qt = [q_ref[:, c * d:(c + 1) * d].T for c in maps]
    slope = [slope_ref[c // 2, :, 0:1] * inv_scale for c in maps]
    tq, tk = q_ref.shape[0], vt_ref.shape[-1]
    nsub = tq // tk
    key, qry = _tile_iotas(tk, tq)
    base = qi * nsub
    rel = (qry - key).astype(F32)
    bias_off = [-slope[2 * h] * rel for h in range(hp)]

    def scores(j):
        return [_dot(_key_block(k_ref, j, tk, c * d, d), qt[c]) for c in maps]

    def diag_scores(sub):
        s = scores(base + sub)
        visible = _chunk_visible(qry, key + sub * tk)
        dist = jnp.abs(rel - float(sub * tk))
        return [jnp.where(visible, s[c] - slope[c] * dist, -1e30) for c in maps]

    s = diag_scores(0)
    m = [_colmax(x) for x in s]
    p = [jnp.exp2((s[c] - m[c]) * c2) for c in maps]
    l = [_colsum(x) for x in p]
    for c in maps:
        acc_ref[c] = _dot(vt_ref[c // 2, base], p[c].astype(BF16))

    def update(u, off, j, m, l):
        m_new = [jnp.maximum(m[c], _colmax(u[c]) + off[c]) for c in maps]
        alpha = [jnp.exp2((m[c] - m_new[c]) * c2) for c in maps]
        p = [jnp.exp2((u[c] - (m_new[c] - off[c])) * c2) for c in maps]
        l = [alpha[c] * l[c] + _colsum(p[c]) for c in maps]
        pv = [_dot(vt_ref[c // 2, j], p[c].astype(BF16)) for c in maps]
        for c in maps:
            acc_ref[c] = alpha[c] * acc_ref[c] + pv[c]
        return m_new, l

    zero = jnp.zeros((1, 1), F32)
    for sub in range(1, nsub):
        m, l = update(diag_scores(sub), [zero] * (2 * hp), base + sub, m, l)

    def body(it, carry):
        j = base - 1 - it
        dist = ((base - j) * tk).astype(F32)
        s = scores(j)
        u = [s[c] + bias_off[c // 2] for c in maps]
        m, l = update(u, [-slope[c] * dist for c in maps], j, carry[:2 * hp], carry[2 * hp:])
        return tuple(m) + tuple(l)

    carry = lax.fori_loop(0, base, body, tuple(m) + tuple(l))
    l = carry[2 * hp:]

    lam = (jnp.exp(jnp.sum(lq1_ref[...] * lk1_ref[...], axis=-1, keepdims=True))
           - jnp.exp(jnp.sum(lq2_ref[...] * lk2_ref[...], axis=-1, keepdims=True))
           + lambda_init)
    for h in range(hp):
        ot = acc_ref[2 * h] / l[2 * h] - lam * (acc_ref[2 * h + 1] / l[2 * h + 1])
        yt = ot * lax.rsqrt(jnp.mean(ot * ot, axis=0, keepdims=True) + EPS)
        o_ref[:, h * dv:(h + 1) * dv] = (yt.T * g_ref[...] * (1.0 - lambda_init)).astype(o_ref.dtype)


def _diff_attention(qkv, slopes, lq1, lk1, lq2, lk2, subln, lambda_init, seq, hp=2, tq=512, tk=512):
    nq = seq // tq
    dh = 2 * DIFF_DIM
    groups = DIFF_HEADS // hp
    vec = pl.BlockSpec((1, DIFF_DIM), lambda b, h, i: (0, 0))
    in_specs = [pl.BlockSpec((tq, hp * dh), lambda b, h, i: (b * nq + i, h)),
                pl.BlockSpec((seq, hp * dh), lambda b, h, i: (b, groups + h)),
                pl.BlockSpec((seq, hp * dh), lambda b, h, i: (b, 2 * groups + h)),
                pl.BlockSpec((hp, 1, LANES), lambda b, h, i: (h, 0, 0)),
                vec, vec, vec, vec,
                pl.BlockSpec((1, dh), lambda b, h, i: (0, 0))]
    inputs = (qkv, qkv, qkv, slopes,
              lq1.reshape(1, -1).astype(F32), lk1.reshape(1, -1).astype(F32),
              lq2.reshape(1, -1).astype(F32), lk2.reshape(1, -1).astype(F32),
              subln.reshape(1, dh).astype(F32))
    return _attn_call(functools.partial(_diff_attn_kernel, lambda_init=lambda_init, hp=hp),
                      inputs, in_specs, dh, dh, 2, hp, tq, tk, seq, DIFF_HEADS, "diff_attention")


def _out_proj_kernel(a1_ref, a2_ref, b_ref, r_ref, o_ref):
    half = a1_ref.shape[1]
    acc = jnp.dot(a1_ref[...], b_ref[:half, :], preferred_element_type=F32)
    acc = acc + jnp.dot(a2_ref[...], b_ref[half:, :], preferred_element_type=F32)
    o_ref[...] = r_ref[...] + acc


def _out_proj(a1, a2, blk1, blk2, b, res, tm=1024, tn=512):
    m = res.shape[0]
    k, n = b.shape
    half = k // 2
    return pl.pallas_call(
        _out_proj_kernel,
        grid=(m // tm, n // tn),
        in_specs=[pl.BlockSpec((tm, half), lambda i, j: (i, blk1)),
                  pl.BlockSpec((tm, half), lambda i, j: (i, blk2)),
                  pl.BlockSpec((k, tn), lambda i, j: (0, j)),
                  pl.BlockSpec((tm, tn), lambda i, j: (i, j))],
        out_specs=pl.BlockSpec((tm, tn), lambda i, j: (i, j)),
        out_shape=jax.ShapeDtypeStruct((m, n), F32),
        compiler_params=_params(("arbitrary", "arbitrary"), 48),
        name="out_proj",
    )(a1, a2, b, res)


def _ffn_up_kernel(a_ref, bg_ref, bu_ref, cwg_ref, cwu_ref, cbg_ref, cbu_ref, o_ref,
                   h_ref, carry_ref, *, tm, tiles_per_seq):
    i = pl.program_id(0)
    j = pl.program_id(1)
    halo = FFN_HALO
    width = 2 * FFN_TN

    @pl.when(i % tiles_per_seq == 0)
    def _():
        h_ref[0:halo, :] = jnp.zeros((halo, width), F32)

    @pl.when(i % tiles_per_seq != 0)
    def _():
        h_ref[0:halo, :] = carry_ref[j]

    a = a_ref[...]
    h_ref[halo:halo + tm, 0:FFN_TN] = _dot(a, bg_ref[...].astype(BF16))
    h_ref[halo:halo + tm, FFN_TN:width] = _dot(a, bu_ref[...].astype(BF16))
    carry_ref[j] = h_ref[tm:tm + halo, :]

    wg, wu = cwg_ref[...], cwu_ref[...]
    w = [jnp.concatenate([wg[t:t + 1], wu[t:t + 1]], axis=1) for t in range(CONV_W)]
    bias = jnp.concatenate([cbg_ref[...], cbu_ref[...]], axis=1)
    rows = 128
    for c in range(tm // rows):
        x = h_ref[c * rows:c * rows + halo + rows, :]
        conv = bias
        for t in range(CONV_W):
            back = CONV_W - 1 - t
            shifted = pltpu.roll(x, back, 0) if back else x
            conv = conv + shifted[halo:, :] * w[t]
        g = conv[:, :FFN_TN]
        u = conv[:, FFN_TN:]
        o_ref[c * rows:(c + 1) * rows, :] = (g * jax.nn.sigmoid(g) * u).astype(o_ref.dtype)


def _ffn_up(a, w_up, conv_w, conv_b, layer, seq, tm=1024):
    m, k = a.shape
    nj = FFN_DIM // FFN_TN
    width = 2 * FFN_TN
    return pl.pallas_call(
        functools.partial(_ffn_up_kernel, tm=tm, tiles_per_seq=seq // tm),
        grid=(m // tm, nj),
        in_specs=[pl.BlockSpec((tm, k), lambda i, j: (i, 0)),
                  pl.BlockSpec((None, k, FFN_TN), lambda i, j: (layer, 0, j)),
                  pl.BlockSpec((None, k, FFN_TN), lambda i, j: (layer, 0, nj + j)),
                  pl.BlockSpec((None, CONV_W, FFN_TN), lambda i, j: (layer, 0, j)),
                  pl.BlockSpec((None, CONV_W, FFN_TN), lambda i, j: (layer, 0, nj + j)),
                  pl.BlockSpec((None, 1, FFN_TN), lambda i, j: (layer, 0, j)),
                  pl.BlockSpec((None, 1, FFN_TN), lambda i, j: (layer, 0, nj + j))],
        out_specs=pl.BlockSpec((tm, FFN_TN), lambda i, j: (i, j)),
        out_shape=jax.ShapeDtypeStruct((m, FFN_DIM), BF16),
        scratch_shapes=[pltpu.VMEM((tm + FFN_HALO, width), F32),
                        pltpu.VMEM((nj, FFN_HALO, width), F32)],
        compiler_params=_params(("arbitrary", "arbitrary"), 48),
        name="ffn_up_conv_gate",
    )(a, w_up, w_up, conv_w, conv_w, conv_b, conv_b)


def _ffn_down_kernel(a_ref, b_ref, r_ref, o_ref):
    o_ref[...] = r_ref[...] + _dot(a_ref[...], b_ref[...].astype(BF16))


def _ffn_down(a, b, res, layer, tm=1024, tn=256):
    m, k = a.shape
    n = b.shape[2]
    return pl.pallas_call(
        _ffn_down_kernel,
        grid=(m // tm, n // tn),
        in_specs=[pl.BlockSpec((tm, k), lambda i, j: (i, 0), pipeline_mode=pl.Buffered(1)),
                  pl.BlockSpec((None, k, tn), lambda i, j: (layer, 0, j)),
                  pl.BlockSpec((tm, tn), lambda i, j: (i, j))],
        out_specs=pl.BlockSpec((tm, tn), lambda i, j: (i, j)),
        out_shape=jax.ShapeDtypeStruct((m, n), F32),
        compiler_params=_params(("arbitrary", "arbitrary"), 56),
        name="ffn_down",
    )(a, b, res)


def _rope_tables(seq):
    half = MLA_ROPE // 2
    inv = ROPE_THETA ** (-jnp.arange(half, dtype=F32) / half)
    ang = jnp.arange(seq).astype(F32)[:, None] * inv[None, :]
    cos, sin = jnp.cos(ang), jnp.sin(ang)
    zero = jnp.zeros_like(cos)
    pad = jnp.zeros((seq, LANES - MLA_ROPE), F32)
    c = jnp.concatenate([cos, cos, pad], axis=1)
    s1 = jnp.concatenate([-sin, zero, pad], axis=1)
    s2 = jnp.concatenate([zero, sin, pad], axis=1)
    return c, s1, s2


def _conv_ffn(h, hn, w_up, conv_w, conv_b, w_down, layer, seq):
    act = _ffn_up(hn, w_up, conv_w, conv_b, layer, seq)
    return _ffn_down(act, w_down, h, layer)


def _mla_sb_layer(h, hn, w_in, q_norm, w_uq, kv_norm, w_ukv, w_out, tabs, seq):
    o1 = MLA_Q_RANK + MLA_KV_RANK
    o2 = o1 + MLA_ROPE
    w_lat = w_in[:, :o1].astype(BF16)
    w_kr = jnp.pad(w_in[:, o1:o2], ((0, 0), (0, LANES - MLA_ROPE))).astype(BF16)
    w_sb = w_in[:, o2:].astype(BF16)
    w_q = jnp.pad(w_uq.reshape(MLA_Q_RANK, MLA_HEADS, MLA_NOPE + MLA_ROPE),
                  ((0, 0), (0, 0), (0, MLA_QK_PAD - MLA_NOPE - MLA_ROPE)))
    w_q = w_q.reshape(MLA_Q_RANK, MLA_HEADS * MLA_QK_PAD).astype(BF16)
    w_kv = w_ukv.reshape(MLA_KV_RANK, MLA_HEADS, 2, MLA_NOPE).transpose(0, 2, 1, 3)
    w_kv = w_kv.reshape(MLA_KV_RANK, 2 * MLA_HEADS * MLA_NOPE).astype(BF16)

    lat = _matmul(hn, w_lat, F32, 1024, 768, "mla_latent_proj")
    krope = _krope(hn, w_kr, tabs)
    sb = _matmul(hn, w_sb, BF16, 1024, 1024, "sb_qkv_proj")
    q = _qup(lat, q_norm, w_q, tabs)
    k, v = _kvup(lat, kv_norm, w_kv, krope)
    o_a = _mla_attention(q, k, v, seq)
    o_b = _sb_attention(sb, seq)
    return _out_proj(o_a, o_b, 0, 0, w_out.astype(BF16), h)


def _diff_layer(h, hn, w_in, lq1, lk1, lq2, lk2, subln, w_out, layer_idx, seq):
    lambda_init = 0.8 - 0.6 * math.exp(-0.3 * layer_idx)
    slopes = jnp.asarray([2.0 ** (-8.0 * (i + 1) / DIFF_HEADS) for i in range(DIFF_HEADS)], F32)
    slopes = jnp.broadcast_to(slopes[:, None, None], (DIFF_HEADS, 1, LANES))
    qkv = _matmul(hn, w_in, BF16, 2048, 512, "diff_qkv_proj", a_buffers=1)
    o = _diff_attention(qkv, slopes, lq1, lk1, lq2, lk2, subln, lambda_init, seq)
    return _out_proj(o, o, 0, 1, w_out.astype(BF16), h)


def kernel(x, norm_mix, norm_ffn, norm_final, ab_w_in, ab_q_norm, ab_w_uq, ab_kv_norm,
           ab_w_ukv, ab_w_out, c_w_in, c_lambda_q1, c_lambda_k1, c_lambda_q2, c_lambda_k2,
           c_subln, c_w_out, ffn_w_up, ffn_conv_w, ffn_conv_b, ffn_w_down):
    batch, seq, d = x.shape
    depth = norm_mix.shape[0]
    tabs = _rope_tables(seq)
    w_up = ffn_w_up
    w_down = ffn_w_down
    conv_w = ffn_conv_w.astype(F32)
    conv_b = ffn_conv_b.astype(F32).reshape(depth, 1, -1)
    h = x.reshape(batch * seq, d)
    for i in range(depth):
        j = i // 2
        hn = _rmsnorm(h, norm_mix[i], BF16)
        if i % 2 == 0:
            h = _mla_sb_layer(h, hn, ab_w_in[j], ab_q_norm[j], ab_w_uq[j], ab_kv_norm[j],
                              ab_w_ukv[j], ab_w_out[j], tabs, seq)
        else:
            h = _diff_layer(h, hn, c_w_in[j], c_lambda_q1[j], c_lambda_k1[j], c_lambda_q2[j],
                            c_lambda_k2[j], c_subln[j], c_w_out[j], i, seq)
        hn = _rmsnorm(h, norm_ffn[i], BF16)
        h = _conv_ffn(h, hn, w_up, conv_w, conv_b, w_down, i, seq)
    return _rmsnorm(h, norm_final, x.dtype).reshape(batch, seq, d)
```

```python
import functools
import math

import jax
import jax.numpy as jnp
from jax import lax
from jax.experimental import pallas as pl
from jax.experimental.pallas import tpu as pltpu

F32 = jnp.float32
BF16 = jnp.bfloat16

CHUNK = 64
EPS = 1e-6
ROPE_THETA = 10000.0
MLA_HEADS = 16
MLA_Q_RANK = 1024
MLA_KV_RANK = 512
MLA_NOPE = 128
MLA_ROPE = 64
MLA_V = 128
SB_HEADS = 16
SB_DIM = 128
DIFF_HEADS = 16
DIFF_DIM = 128
FFN_DIM = 11008
CONV_W = 3

LANES = 128
MLA_QK_PAD = 256
FFN_TN = 256
FFN_HALO = 8
SB_CUM = 256
LOG2E = math.log2(math.e)
MIB = 1024 * 1024


def _params(sem, vmem_mib):
    return pltpu.CompilerParams(dimension_semantics=sem, vmem_limit_bytes=vmem_mib * MIB)


def _rms(x, g):
    return x * lax.rsqrt(jnp.mean(x * x, axis=-1, keepdims=True) + EPS) * g


def _rope_slab(x, c, s1, s2):
    return x * c + pltpu.roll(x, 96, 1) * s1 + pltpu.roll(x, 32, 1) * s2


def _rmsnorm_kernel(x_ref, g_ref, o_ref):
    o_ref[...] = _rms(x_ref[...].astype(F32), g_ref[...]).astype(o_ref.dtype)


def _rmsnorm(x, g, out_dtype, tm=512):
    m, d = x.shape
    return pl.pallas_call(
        _rmsnorm_kernel,
        grid=(m // tm,),
        in_specs=[pl.BlockSpec((tm, d), lambda i: (i, 0)),
                  pl.BlockSpec((1, d), lambda i: (0, 0))],
        out_specs=pl.BlockSpec((tm, d), lambda i: (i, 0)),
        out_shape=jax.ShapeDtypeStruct((m, d), out_dtype),
        compiler_params=_params(("arbitrary",), 48),
        name="rmsnorm",
    )(x, g.reshape(1, d).astype(F32))


def _mm_kernel(a_ref, b_ref, o_ref):
    o_ref[...] = jnp.dot(a_ref[...], b_ref[...].astype(BF16),
                         preferred_element_type=F32).astype(o_ref.dtype)


def _matmul(a, b, out_dtype, tm, tn, name, a_buffers=2):
    m, k = a.shape
    n = b.shape[1]
    return pl.pallas_call(
        _mm_kernel,
        grid=(m // tm, n // tn),
        in_specs=[pl.BlockSpec((tm, k), lambda i, j: (i, 0),
                               pipeline_mode=pl.Buffered(a_buffers)),
                  pl.BlockSpec((k, tn), lambda i, j: (0, j))],
        out_specs=pl.BlockSpec((tm, tn), lambda i, j: (i, j)),
        out_shape=jax.ShapeDtypeStruct((m, n), out_dtype),
        compiler_params=_params(("arbitrary", "arbitrary"), 56),
        name=name,
    )(a, b)


def _latent_kernel(a_ref, b_ref, bk_ref, c_ref, s1_ref, s2_ref, o_ref, kr_ref):
    a = a_ref[...]
    o_ref[...] = _dot(a, b_ref[...])

    @pl.when(pl.program_id(1) == 0)
    def _():
        kr_ref[...] = _rope_slab(_dot(a, bk_ref[...]), c_ref[...], s1_ref[...],
                                 s2_ref[...]).astype(kr_ref.dtype)


def _latent_and_krope(a, b, bk, tabs, tm=1024, tn=768):
    m, k = a.shape
    n = b.shape[1]
    nt = tabs[0].shape[0] // tm
    tab_spec = pl.BlockSpec((tm, LANES), lambda i, j: (i % nt, 0))
    return pl.pallas_call(
        _latent_kernel,
        grid=(m // tm, n // tn),
        in_specs=[pl.BlockSpec((tm, k), lambda i, j: (i, 0)),
                  pl.BlockSpec((k, tn), lambda i, j: (0, j)),
                  pl.BlockSpec((k, LANES), lambda i, j: (0, 0)),
                  tab_spec, tab_spec, tab_spec],
        out_specs=[pl.BlockSpec((tm, tn), lambda i, j: (i, j)),
                   pl.BlockSpec((tm, LANES), lambda i, j: (i, 0))],
        out_shape=[jax.ShapeDtypeStruct((m, n), F32),
                   jax.ShapeDtypeStruct((m, LANES), BF16)],
        compiler_params=_params(("arbitrary", "arbitrary"), 48),
        name="mla_latent_krope_proj",
    )(a, b, bk, *tabs)


def _qup_kernel(a_ref, g_ref, b_ref, c_ref, s1_ref, s2_ref, o_ref, an_ref, *, heads_per_tile):
    @pl.when(pl.program_id(1) == 0)
    def _():
        an_ref[...] = _rms(a_ref[...], g_ref[...]).astype(BF16)

    acc = jnp.dot(an_ref[...], b_ref[...], preferred_element_type=F32)
    c, s1, s2 = c_ref[...], s1_ref[...], s2_ref[...]
    for h in range(heads_per_tile):
        lo = h * MLA_QK_PAD
        o_ref[:, lo:lo + LANES] = acc[:, lo:lo + LANES].astype(o_ref.dtype)
        o_ref[:, lo + LANES:lo + 2 * LANES] = _rope_slab(
            acc[:, lo + LANES:lo + 2 * LANES], c, s1, s2).astype(o_ref.dtype)


def _qup(lat, g, w, tabs, tm=1024, tn=1024):
    m = lat.shape[0]
    k = MLA_Q_RANK
    n = w.shape[1]
    nt = tabs[0].shape[0] // tm
    tab_spec = pl.BlockSpec((tm, LANES), lambda i, j: (i % nt, 0))
    return pl.pallas_call(
        functools.partial(_qup_kernel, heads_per_tile=tn // MLA_QK_PAD),
        grid=(m // tm, n // tn),
        in_specs=[pl.BlockSpec((tm, k), lambda i, j: (i, 0)),
                  pl.BlockSpec((1, k), lambda i, j: (0, 0)),
                  pl.BlockSpec((k, tn), lambda i, j: (0, j)),
                  tab_spec, tab_spec, tab_spec],
        out_specs=pl.BlockSpec((tm, tn), lambda i, j: (i, j)),
        out_shape=jax.ShapeDtypeStruct((m, n), BF16),
        scratch_shapes=[pltpu.VMEM((tm, k), BF16)],
        compiler_params=_params(("arbitrary", "arbitrary"), 48),
        name="mla_q_up",
    )(lat, g.reshape(1, k).astype(F32), w, *tabs)


def _kvup_kernel(a_ref, g_ref, b_ref, kr_ref, k_ref, v_ref):
    xn = _rms(a_ref[...], g_ref[...]).astype(BF16)
    acc = jnp.dot(xn, b_ref[...], preferred_element_type=F32)
    kr = kr_ref[...]
    for h in range(MLA_HEADS):
        lo = h * MLA_QK_PAD
        k_ref[:, lo:lo + LANES] = acc[:, h * LANES:(h + 1) * LANES].astype(k_ref.dtype)
        k_ref[:, lo + LANES:lo + 2 * LANES] = kr
    v_ref[...] = acc[:, MLA_HEADS * MLA_NOPE:].astype(v_ref.dtype)


def _kvup(lat, g, w, krope, tm=512):
    m = lat.shape[0]
    k = MLA_KV_RANK
    n = w.shape[1]
    lat_blk = MLA_Q_RANK // MLA_KV_RANK
    return pl.pallas_call(
        _kvup_kernel,
        grid=(m // tm,),
        in_specs=[pl.BlockSpec((tm, k), lambda i: (i, lat_blk)),
                  pl.BlockSpec((1, k), lambda i: (0, 0)),
                  pl.BlockSpec((k, n), lambda i: (0, 0)),
                  pl.BlockSpec((tm, LANES), lambda i: (i, 0))],
        out_specs=[pl.BlockSpec((tm, MLA_HEADS * MLA_QK_PAD), lambda i: (i, 0)),
                   pl.BlockSpec((tm, MLA_HEADS * MLA_V), lambda i: (i, 0))],
        out_shape=[jax.ShapeDtypeStruct((m, MLA_HEADS * MLA_QK_PAD), BF16),
                   jax.ShapeDtypeStruct((m, MLA_HEADS * MLA_V), BF16)],
        compiler_params=_params(("arbitrary",), 48),
        name="mla_kv_up",
    )(lat, g.reshape(1, k).astype(F32), w, krope)


def _tile_iotas(tk, tq):
    key = lax.broadcasted_iota(jnp.int32, (tk, tq), 0)
    qry = lax.broadcasted_iota(jnp.int32, (tk, tq), 1)
    return key, qry


def _chunk_visible(qry, key):
    shift = CHUNK.bit_length() - 1
    return jnp.right_shift(key, shift) <= jnp.right_shift(qry, shift)


def _stage_vt(v_ref, vt_ref, hp, dv):
    tk = vt_ref.shape[-1]
    for h in range(hp):
        for c in range(v_ref.shape[0] // tk):
            vt_ref[h, c] = v_ref[c * tk:(c + 1) * tk, h * dv:(h + 1) * dv].T


def _key_block(k_ref, j, tk, lo, width):
    start = pl.multiple_of(j * tk, tk)
    return k_ref[pl.ds(start, tk), lo:lo + width]


def _dot(a, b):
    return jnp.dot(a, b, preferred_element_type=F32)


def _neg_abs(x):
    bits = lax.bitcast_convert_type(x, jnp.uint32) | jnp.uint32(0x80000000)
    return lax.bitcast_convert_type(bits, F32)


def _colmax(x):
    return jnp.max(x, axis=0, keepdims=True)


def _colsum(x):
    return jnp.sum(x, axis=0, keepdims=True)


def _attn_call(kernel_fn, inputs, in_specs, d_out, dv, n_acc, hp, tq, tk, seq, heads, name):
    m = inputs[0].shape[0]
    nq = seq // tq
    return pl.pallas_call(
        kernel_fn,
        grid=(m // seq, heads // hp, nq),
        in_specs=in_specs,
        out_specs=pl.BlockSpec((tq, hp * d_out), lambda b, h, i: (b * nq + i, h)),
        out_shape=jax.ShapeDtypeStruct((m, heads * d_out), BF16),
        scratch_shapes=[pltpu.VMEM((hp, seq // tk, dv, tk), BF16),
                        pltpu.VMEM((n_acc * hp, dv, tq), F32)],
        compiler_params=_params(("arbitrary", "arbitrary", "arbitrary"), 56),
        name=name,
    )(*inputs)


def _mla_attn_kernel(q_ref, k_ref, v_ref, o_ref, vt_ref, acc_ref, *, hp):
    qi = pl.program_id(2)
    dq, dv = MLA_QK_PAD, MLA_V
    heads = range(hp)

    @pl.when(qi == 0)
    def _():
        _stage_vt(v_ref, vt_ref, hp, dv)

    c = LOG2E / math.sqrt(MLA_NOPE + MLA_ROPE)
    qt = [q_ref[:, h * dq:(h + 1) * dq].T for h in heads]
    tq, tk = q_ref.shape[0], vt_ref.shape[-1]
    nsub = tq // tk
    key, qry = _tile_iotas(tk, tq)
    base = qi * nsub

    def scores(j, key_offset=None):
        s = [_dot(_key_block(k_ref, j, tk, h *dq, dq), qt[h]) for h in heads]
        if key_offset is None:
            return s
        visible = _chunk_visible(qry, key + key_offset)
        return [jnp.where(visible, x, -1e30) for x in s]

    s = scores(base, 0)
    m = [_colmax(x) for x in s]
    p = [jnp.exp2((s[h] - m[h]) * c) for h in heads]
    l = [_colsum(x) for x in p]
    for h in heads:
        acc_ref[h] = _dot(vt_ref[h, base], p[h].astype(BF16))

    def update(s, j, m, l):
        m_new = [jnp.maximum(m[h], _colmax(s[h])) for h in heads]
        alpha = [jnp.exp2((m[h] - m_new[h]) * c) for h in heads]
        p = [jnp.exp2((s[h] - m_new[h]) * c) for h in heads]
        l = [alpha[h] * l[h] + _colsum(p[h]) for h in heads]
        pv = [_dot(vt_ref[h, j], p[h].astype(BF16)) for h in heads]
        for h in heads:
            acc_ref[h] = alpha[h] * acc_ref[h] + pv[h]
        return m_new, l

    for sub in range(1, nsub):
        m, l = update(scores(base + sub, sub * tk), base + sub, m, l)

    def body(j, carry):
        m, l = update(scores(j), j, carry[:hp], carry[hp:])
        return tuple(m) + tuple(l)

    carry = lax.fori_loop(0, base, body, tuple(m) + tuple(l))
    l = carry[hp:]
    for h in heads:
        o_ref[:, h * dv:(h + 1) * dv] = (acc_ref[h] / l[h]).T.astype(o_ref.dtype)


def _mla_attention(q, k, v, seq, hp=4, tq=512, tk=512):
    nq = seq // tq
    in_specs = [pl.BlockSpec((tq, hp * MLA_QK_PAD), lambda b, h, i: (b * nq + i, h)),
                pl.BlockSpec((seq, hp * MLA_QK_PAD), lambda b, h, i: (b, h)),
                pl.BlockSpec((seq, hp * MLA_V), lambda b, h, i: (b, h))]
    return _attn_call(functools.partial(_mla_attn_kernel, hp=hp), (q, k, v), in_specs,
                      MLA_V, MLA_V, 1, hp, tq, tk, seq, MLA_HEADS, "mla_attention")


def _sb_attn_kernel(q_ref, k_ref, v_ref, o_ref, vt_ref, acc_ref, *, hp):
    qi = pl.program_id(2)
    d = SB_DIM
    heads = range(hp)

    @pl.when(qi == 0)
    def _():
        _stage_vt(v_ref, vt_ref, hp, d)

    scale = 1.0 / math.sqrt(SB_DIM)
    qt = [q_ref[:, h * d:(h + 1) * d].T for h in heads]
    tq, tk = q_ref.shape[0], vt_ref.shape[-1]
    nsub = tq // tk
    key, qry = _tile_iotas(tk, tq)
    base = qi * nsub
    later = (lax.broadcasted_iota(jnp.int32, (SB_CUM, SB_CUM), 1)
             > lax.broadcasted_iota(jnp.int32, (SB_CUM, SB_CUM), 0)).astype(BF16)
    later2 = jnp.concatenate([later, later], axis=1)

    def cumsum_later(x):
        pieces, total = [], None
        for c in reversed(range(tk // SB_CUM)):
            xc = x[c * SB_CUM:(c + 1) * SB_CUM, :]
            hi = xc.astype(BF16)
            lo = (xc - hi.astype(F32)).astype(BF16)
            local = _dot(later2, jnp.concatenate([hi, lo], axis=0))
            if total is not None:
                local = local + total
            total = local[0:1, :] + xc[0:1, :]
            pieces.append(local)
        return pieces[0] if len(pieces) == 1 else jnp.concatenate(pieces[::-1], axis=0)

    def block(j, r, key_offset=None):
        diagonal = key_offset is not None
        if diagonal:
            strict = key + key_offset < qry
        z2 = [_dot(_key_block(k_ref, j, tk, h * d, d), qt[h]) * (scale * LOG2E) for h in heads]
        t = [jnp.log2(1.0 + jnp.exp2(_neg_abs(x))) for x in z2]
        neg_keep = [jnp.maximum(z2[h], 0.0) + t[h] for h in heads]
        log_beta = [z2[h] - neg_keep[h] for h in heads]
        if diagonal:
            neg_keep = [jnp.where(strict, x, 0.0) for x in neg_keep]
        after = [cumsum_later(x) for x in neg_keep]
        a = [jnp.exp2(log_beta[h] - after[h] - r[h]) for h in heads]
        if diagonal:
            a = [jnp.where(strict, x, 0.0) for x in a]
        pv = [_dot(vt_ref[h, j], a[h].astype(BF16)) for h in heads]
        r = [r[h] + after[h][0:1, :] + neg_keep[h][0:1, :] for h in heads]
        return r, pv

    r = [jnp.zeros((1, tq), F32)] * hp
    for sub in reversed(range(nsub)):
        r, pv = block(base + sub, r, sub * tk)
        for h in heads:
            if sub == nsub - 1:
                acc_ref[h] = pv[h]
            else:
                acc_ref[h] += pv[h]

    def body(it, r):
        r, pv = block(base - 1 - it, list(r))
        for h in heads:
            acc_ref[h] += pv[h]
        return tuple(r)

    lax.fori_loop(0, base, body, tuple(r))
    for h in heads:
        o_ref[:, h * d:(h + 1) * d] = acc_ref[h].T.astype(o_ref.dtype)


def _sb_attention(qkv, seq, hp=4, tq=512, tk=512):
    nq = seq // tq
    groups = SB_HEADS // hp
    in_specs = [pl.BlockSpec((tq, hp * SB_DIM), lambda b, h, i: (b * nq + i, h)),
                pl.BlockSpec((seq, hp * SB_DIM), lambda b, h, i: (b, groups + h)),
                pl.BlockSpec((seq, hp * SB_DIM), lambda b, h, i: (b, 2 * groups + h))]
    return _attn_call(functools.partial(_sb_attn_kernel, hp=hp), (qkv, qkv, qkv), in_specs,
                      SB_DIM, SB_DIM, 1, hp, tq, tk, seq, SB_HEADS, "sb_attention")


def _diff_attn_kernel(q_ref, k_ref, v_ref, slope_ref, lq1_ref, lk1_ref, lq2_ref, lk2_ref,
                      g_ref, o_ref, vt_ref, acc_ref, *, lambda_init, hp):
    qi = pl.program_id(2)
    d = DIFF_DIM
    dv = 2 * d
    maps = range(2 * hp)

    @pl.when(qi == 0)
    def _():
        _stage_vt(v_ref, vt_ref, hp, dv)

    inv_scale = math.sqrt(DIFF_DIM)
    c2 = LOG2E / inv_scale
    qt = [q_ref[:, c * d:(c + 1) * d].T for c in maps]
    slope = [slope_ref[c // 2, :, 0:1] * inv_scale for c in maps]
    tq, tk = q_ref.shape[0], vt_ref.shape[-1]
    nsub = tq // tk
    key, qry = _tile_iotas(tk, tq)
    base = qi * nsub
    rel = (qry - key).astype(F32)
    bias_off = [-slope[2 * h] * rel for h in range(hp)]

    def scores(j):
        return [_dot(_key_block(k_ref, j, tk, c * d, d), qt[c]) for c in maps]

    def diag_scores(sub):
        s = scores(base + sub)
        visible = _chunk_visible(qry, key + sub * tk)
        dist = jnp.abs(rel - float(sub * tk))
        return [jnp.where(visible, s[c] - slope[c] * dist, -1e30) for c in maps]

    s = diag_scores(0)
    m = [_colmax(x) for x in s]
    p = [jnp.exp2((s[c] - m[c]) * c2) for c in maps]
    l = [_colsum(x) for x in p]
    for c in maps:
        acc_ref[c] = _dot(vt_ref[c // 2, base], p[c].astype(BF16))

    def update(u, off, j, m, l):
        m_new = [jnp.maximum(m[c], _colmax(u[c]) + off[c]) for c in maps]
        alpha = [jnp.exp2((m[c] - m_new[c]) * c2) for c in maps]
        p = [jnp.exp2((u[c] - (m_new[c] - off[c])) * c2) for c in maps]
        l = [alpha[c] * l[c] + _colsum(p[c]) for c in maps]
        pv = [_dot(vt_ref[c // 2, j], p[c].astype(BF16)) for c in maps]
        for c in maps:
            acc_ref[c] = alpha[c] * acc_ref[c] + pv[c]
        return m_new, l

    zero = jnp.zeros((1, 1), F32)
    for sub in range(1, nsub):
        m, l = update(diag_scores(sub), [zero] * (2 * hp), base + sub, m, l)

    def body(it, carry):
        j = base - 1 - it
        dist = ((base - j) * tk).astype(F32)
        s = scores(j)
        u = [s[c] + bias_off[c // 2] for c in maps]
        m, l = update(u, [-slope[c] * dist for c in maps], j, carry[:2 * hp], carry[2 * hp:])
        return tuple(m) + tuple(l)

    carry = lax.fori_loop(0, base, body, tuple(m) + tuple(l))
    l = carry[2 * hp:]

    lam = (jnp.exp(jnp.sum(lq1_ref[...] * lk1_ref[...], axis=-1, keepdims=True))
           - jnp.exp(jnp.sum(lq2_ref[...] * lk2_ref[...], axis=-1, keepdims=True))
           + lambda_init)
    for h in range(hp):
        ot = acc_ref[2 * h] / l[2 * h] - lam * (acc_ref[2 * h + 1] / l[2 * h + 1])
        yt = ot * lax.rsqrt(jnp.mean(ot * ot, axis=0, keepdims=True) + EPS)
        o_ref[:, h * dv:(h + 1) * dv] = (yt.T * g_ref[...] * (1.0 - lambda_init)).astype(o_ref.dtype)


def _diff_attention(qkv, slopes, lq1, lk1, lq2, lk2, subln, lambda_init, seq, hp=2, tq=512, tk=512):
    nq = seq // tq
    dh = 2 * DIFF_DIM
    groups = DIFF_HEADS // hp
    vec = pl.BlockSpec((1, DIFF_DIM), lambda b, h, i: (0, 0))
    in_specs = [pl.BlockSpec((tq, hp * dh), lambda b, h, i: (b * nq + i, h)),
                pl.BlockSpec((seq, hp * dh), lambda b, h, i: (b, groups + h)),
                pl.BlockSpec((seq, hp * dh), lambda b, h, i: (b, 2 * groups + h)),
                pl.BlockSpec((hp, 1, LANES), lambda b, h, i: (h, 0, 0)),
                vec, vec, vec, vec,
                pl.BlockSpec((1, dh), lambda b, h, i: (0, 0))]
    inputs = (qkv, qkv, qkv, slopes,
              lq1.reshape(1, -1).astype(F32), lk1.reshape(1, -1).astype(F32),
              lq2.reshape(1, -1).astype(F32), lk2.reshape(1, -1).astype(F32),
              subln.reshape(1, dh).astype(F32))
    return _attn_call(functools.partial(_diff_attn_kernel, lambda_init=lambda_init, hp=hp),
                      inputs, in_specs, dh, dh, 2, hp, tq, tk, seq, DIFF_HEADS, "diff_attention")


def _out_proj_kernel(a1_ref, a2_ref, b_ref, r_ref, o_ref):
    half = a1_ref.shape[1]
    acc = jnp.dot(a1_ref[...], b_ref[:half, :], preferred_element_type=F32)
    acc = acc + jnp.dot(a2_ref[...], b_ref[half:, :], preferred_element_type=F32)
    o_ref[...] = r_ref[...] + acc


def _out_proj(a1, a2, blk1, blk2, b, res, tm=1024, tn=512):
    m = res.shape[0]
    k, n = b.shape
    half = k // 2
    return pl.pallas_call(
        _out_proj_kernel,
        grid=(m // tm, n // tn),
        in_specs=[pl.BlockSpec((tm, half), lambda i, j: (i, blk1)),
                  pl.BlockSpec((tm, half), lambda i, j: (i, blk2)),
                  pl.BlockSpec((k, tn), lambda i, j: (0, j)),
                  pl.BlockSpec((tm, tn), lambda i, j: (i, j))],
        out_specs=pl.BlockSpec((tm, tn), lambda i, j: (i, j)),
        out_shape=jax.ShapeDtypeStruct((m, n), F32),
        compiler_params=_params(("arbitrary", "arbitrary"), 48),
        name="out_proj",
    )(a1, a2, b, res)


def _ffn_up_kernel(a_ref, bg_ref, bu_ref, cwg_ref, cwu_ref, cbg_ref, cbu_ref, o_ref,
                   h_ref, carry_ref, *, tm, tiles_per_seq):
    i = pl.program_id(0)
    j = pl.program_id(1)
    halo = FFN_HALO
    width = 2 * FFN_TN

    @pl.when(i % tiles_per_seq == 0)
    def _():
        h_ref[0:halo, :] = jnp.zeros((halo, width), F32)

    @pl.when(i % tiles_per_seq != 0)
    def _():
        h_ref[0:halo, :] = carry_ref[j]

    a = a_ref[...]
    h_ref[halo:halo + tm, 0:FFN_TN] = _dot(a, bg_ref[...].astype(BF16))
    h_ref[halo:halo + tm, FFN_TN:width] = _dot(a, bu_ref[...].astype(BF16))
    carry_ref[j] = h_ref[tm:tm + halo, :]

    wg, wu = cwg_ref[...], cwu_ref[...]
    w = [jnp.concatenate([wg[t:t + 1], wu[t:t + 1]], axis=1) for t in range(CONV_W)]
    bias = jnp.concatenate([cbg_ref[...], cbu_ref[...]], axis=1)
    rows = 128
    for c in range(tm // rows):
        x = h_ref[c * rows:c * rows + halo + rows, :]
        conv = bias
        for t in range(CONV_W):
            back = CONV_W - 1 - t
            shifted = pltpu.roll(x, back, 0) if back else x
            conv = conv + shifted[halo:, :] * w[t]
        g = conv[:, :FFN_TN]
        u = conv[:, FFN_TN:]
        o_ref[c * rows:(c + 1) * rows, :] = (g * jax.nn.sigmoid(g) * u).astype(o_ref.dtype)


def _ffn_up(a, w_up, conv_w, conv_b, layer, seq, tm=1024):
    m, k = a.shape
    nj = FFN_DIM // FFN_TN
    width = 2 * FFN_TN
    return pl.pallas_call(
        functools.partial(_ffn_up_kernel, tm=tm, tiles_per_seq=seq // tm),
        grid=(m // tm, nj),
        in_specs=[pl.BlockSpec((tm, k), lambda i, j: (i, 0)),
                  pl.BlockSpec((None, k, FFN_TN), lambda i, j: (layer, 0, j)),
                  pl.BlockSpec((None, k, FFN_TN), lambda i, j: (layer, 0, nj + j)),
                  pl.BlockSpec((None, CONV_W, FFN_TN), lambda i, j: (layer, 0, j)),
                  pl.BlockSpec((None, CONV_W, FFN_TN), lambda i, j: (layer, 0, nj + j)),
                  pl.BlockSpec((None, 1, FFN_TN), lambda i, j: (layer, 0, j)),
                  pl.BlockSpec((None, 1, FFN_TN), lambda i, j: (layer, 0, nj + j))],
        out_specs=pl.BlockSpec((tm, FFN_TN), lambda i, j: (i, j)),
        out_shape=jax.ShapeDtypeStruct((m, FFN_DIM), BF16),
        scratch_shapes=[pltpu.VMEM((tm + FFN_HALO, width), F32),
                        pltpu.VMEM((nj, FFN_HALO, width), F32)],
        compiler_params=_params(("arbitrary", "arbitrary"), 48),
        name="ffn_up_conv_gate",
    )(a, w_up, w_up, conv_w, conv_w, conv_b, conv_b)


def _ffn_down_kernel(a_ref, b_ref, r_ref, o_ref):
    o_ref[...] = r_ref[...] + _dot(a_ref[...], b_ref[...].astype(BF16))


def _ffn_down(a, b, res, layer, tm=1024, tn=256):
    m, k = a.shape
    n = b.shape[2]
    return pl.pallas_call(
        _ffn_down_kernel,
        grid=(m // tm, n // tn),
        in_specs=[pl.BlockSpec((tm, k), lambda i, j: (i, 0), pipeline_mode=pl.Buffered(1)),
                  pl.BlockSpec((None, k, tn), lambda i, j: (layer, 0, j)),
                  pl.BlockSpec((tm, tn), lambda i, j: (i, j))],
        out_specs=pl.BlockSpec((tm, tn), lambda i, j: (i, j)),
        out_shape=jax.ShapeDtypeStruct((m, n), F32),
        compiler_params=_params(("arbitrary", "arbitrary"), 56),
        name="ffn_down",
    )(a, b, res)


def _rope_tables(seq):
    half = MLA_ROPE // 2
    inv = ROPE_THETA ** (-jnp.arange(half, dtype=F32) / half)
    ang = jnp.arange(seq).astype(F32)[:, None] * inv[None, :]
    cos, sin = jnp.cos(ang), jnp.sin(ang)
    zero = jnp.zeros_like(cos)
    pad = jnp.zeros((seq, LANES - MLA_ROPE), F32)
    c = jnp.concatenate([cos, cos, pad], axis=1)
    s1 = jnp.concatenate([-sin, zero, pad], axis=1)
    s2 = jnp.concatenate([zero, sin, pad], axis=1)
    return c, s1, s2


def _conv_ffn(h, hn, w_up, conv_w, conv_b, w_down, layer, seq):
    act = _ffn_up(hn, w_up, conv_w, conv_b, layer, seq)
    return _ffn_down(act, w_down, h, layer)


def _mla_sb_layer(h, hn, w_in, q_norm, w_uq, kv_norm, w_ukv, w_out, tabs, seq):
    o1 = MLA_Q_RANK + MLA_KV_RANK
    o2 = o1 + MLA_ROPE
    w_lat = w_in[:, :o1].astype(BF16)
    w_kr = jnp.pad(w_in[:, o1:o2], ((0, 0), (0, LANES - MLA_ROPE))).astype(BF16)
    w_sb = w_in[:, o2:].astype(BF16)
    w_q = jnp.pad(w_uq.reshape(MLA_Q_RANK, MLA_HEADS, MLA_NOPE + MLA_ROPE),
                  ((0, 0), (0, 0), (0, MLA_QK_PAD - MLA_NOPE - MLA_ROPE)))
    w_q = w_q.reshape(MLA_Q_RANK, MLA_HEADS * MLA_QK_PAD).astype(BF16)
    w_kv = w_ukv.reshape(MLA_KV_RANK, MLA_HEADS, 2, MLA_NOPE).transpose(0, 2, 1, 3)
    w_kv = w_kv.reshape(MLA_KV_RANK, 2 * MLA_HEADS * MLA_NOPE).astype(BF16)

    lat, krope = _latent_and_krope(hn, w_lat, w_kr, tabs)
    sb = _matmul(hn, w_sb, BF16, 1024, 1024, "sb_qkv_proj")
    q = _qup(lat, q_norm, w_q, tabs)
    k, v = _kvup(lat, kv_norm, w_kv, krope)
    o_a = _mla_attention(q, k, v, seq)
    o_b = _sb_attention(sb, seq)
    return _out_proj(o_a, o_b, 0, 0, w_out.astype(BF16), h)


def _diff_layer(h, hn, w_in, lq1, lk1, lq2, lk2, subln, w_out, layer_idx, seq):
    lambda_init = 0.8 - 0.6 * math.exp(-0.3 * layer_idx)
    slopes = jnp.asarray([2.0 ** (-8.0 * (i + 1) / DIFF_HEADS) for i in range(DIFF_HEADS)], F32)
    slopes = jnp.broadcast_to(slopes[:, None, None], (DIFF_HEADS, 1, LANES))
    qkv = _matmul(hn, w_in, BF16, 2048, 512, "diff_qkv_proj", a_buffers=1)
    o = _diff_attention(qkv, slopes, lq1, lk1, lq2, lk2, subln, lambda_init, seq)
    return _out_proj(o, o, 0, 1, w_out.astype(BF16), h)


def kernel(x, norm_mix, norm_ffn, norm_final, ab_w_in, ab_q_norm, ab_w_uq, ab_kv_norm,
           ab_w_ukv, ab_w_out, c_w_in, c_lambda_q1, c_lambda_k1, c_lambda_q2, c_lambda_k2,
           c_subln, c_w_out, ffn_w_up, ffn_conv_w, ffn_conv_b, ffn_w_down):
    batch, seq, d = x.shape
    depth = norm_mix.shape[0]
    tabs = _rope_tables(seq)
    w_up = ffn_w_up
    w_down = ffn_w_down
    conv_w = ffn_conv_w.astype(F32)
    conv_b = ffn_conv_b.astype(F32).reshape(depth, 1, -1)
    h = x.reshape(batch * seq, d)
    for i in range(depth):
        j = i // 2
        hn = _rmsnorm(h, norm_mix[i], BF16)
        if i % 2 == 0:
            h = _mla_sb_layer(h, hn, ab_w_in[j], ab_q_norm[j], ab_w_uq[j], ab_kv_norm[j],
                              ab_w_ukv[j], ab_w_out[j], tabs, seq)
        else:
            h = _diff_layer(h, hn, c_w_in[j], c_lambda_q1[j], c_lambda_k1[j], c_lambda_q2[j],
                            c_lambda_k2[j], c_subln[j], c_w_out[j], i, seq)
        hn = _rmsnorm(h, norm_ffn[i], BF16)
        h = _conv_ffn(h, hn, w_up, conv_w, conv_b, w_down, i, seq)
    return _rmsnorm(h, norm_final, x.dtype).reshape(batch, seq, d)
```
